```python
import math
import jax, jax.numpy as jnp
from jax import lax
import numpy as np

D_MODEL = 1024
BATCH = 4
SEQ = 4096
DEPTH = 1
DEC_BATCH = 16
DEC_SEQ = 4096
PAST_LEN = 128

GRID_W = 64
POOL_WINDOWS = (2, 4, 8, 16)
POOL_GROUP = 128
POOL_WIDTH = POOL_GROUP * len(POOL_WINDOWS)
N_HEADS = 8
N_KV_HEADS = 2
HEAD_DIM = 64
Q_WIDTH = N_HEADS * HEAD_DIM
KV_WIDTH = N_KV_HEADS * HEAD_DIM
ROPE_THETA = 10000.0
Q_BLOCK = 128
IN_WIDTH = POOL_WIDTH + Q_WIDTH + 2 * KV_WIDTH + 2 * D_MODEL
N_KEYS = 128
N_EXPERTS = N_KEYS * N_KEYS
PEER_HEADS = 8
PEER_QDIM = 256
PEER_HALF = PEER_QDIM // 2
PEER_TOPK = 16
TOKEN_CHUNK = 128
EPS = 1e-6

kernel_name = "hybrid_pool_gqa_peer_encoder"


def rmsnorm(x, g):
    xf = x.astype(jnp.float32)
    r = lax.rsqrt(jnp.mean(xf * xf, axis=-1, keepdims=True) + EPS)
    return (xf * r).astype(x.dtype) * g


def pool_mixer(p, pool_w, pool_scale):
    B, T, _ = p.shape
    t = jnp.arange(T)
    outs = []
    for gi, w in enumerate(POOL_WINDOWS):
        xg = p[..., gi * POOL_GROUP:(gi + 1) * POOL_GROUP].astype(jnp.float32)
        cs = jnp.concatenate([jnp.zeros((B, 1, POOL_GROUP), jnp.float32), jnp.cumsum(xg, axis=1)], axis=1)
        lo = jnp.clip(t - w // 2, 0, T)
        hi = jnp.clip(t + w - w // 2, 0, T)
        cnt = (hi - lo).astype(jnp.float32)[None, :, None]
        mean = (jnp.take(cs, hi, axis=1) - jnp.take(cs, lo, axis=1)) / cnt
        outs.append(mean - xg)
    pooled = jnp.stack(outs, axis=2).astype(p.dtype)
    mixed = jnp.einsum('btgc,gcd->btgd', pooled, pool_w).reshape(B, T, POOL_WIDTH)
    return mixed * pool_scale


def axial_rope(x, rows, cols):
    half = HEAD_DIM // 2
    quarter = half // 2
    inv_freq = ROPE_THETA ** (-jnp.arange(quarter, dtype=jnp.float32) / quarter)

    def rot(xs, pos):
        ang = pos.astype(jnp.float32)[:, None] * inv_freq
        cos = jnp.cos(ang)[:, None, :]
        sin = jnp.sin(ang)[:, None, :]
        a = xs[..., :quarter].astype(jnp.float32)
        b = xs[..., quarter:].astype(jnp.float32)
        return jnp.concatenate([a * cos - b * sin, b * cos + a * sin], axis=-1)

    out = jnp.concatenate([rot(x[..., :half], rows), rot(x[..., half:], cols)], axis=-1)
    return out.astype(x.dtype)


def gqa_attention(q, k, v):
    B, T = q.shape[0], q.shape[1]
    nb = T // Q_BLOCK
    grp = N_HEADS // N_KV_HEADS
    qb = q.reshape(B, nb, Q_BLOCK, N_KV_HEADS, grp, HEAD_DIM).transpose(1, 0, 2, 3, 4, 5)
    scale = HEAD_DIM ** -0.5

    def block(qblk):
        s = jnp.einsum('bqkgd,bskd->bkgqs', qblk, k).astype(jnp.float32) * scale
        pr = jax.nn.softmax(s, axis=-1).astype(v.dtype)
        return jnp.einsum('bkgqs,bskd->bqkgd', pr, v)

    o = lax.map(block, qb)
    return o.transpose(1, 0, 2, 3, 4, 5).reshape(B, T, Q_WIDTH)


def peer(h, w_query, sub_keys, expert_u, expert_v):
    B, T, D = h.shape
    n = B * T
    hc = h.reshape(n // TOKEN_CHUNK, TOKEN_CHUNK, D)

    def chunk(xc):
        C = xc.shape[0]
        q = (xc @ w_query).reshape(C, PEER_HEADS, 2, PEER_HALF)
        s = jnp.einsum('chpd,hpkd->chpk', q, sub_keys).astype(jnp.float32)
        sv, si = lax.top_k(s, PEER_TOPK)
        cand = (sv[:, :, 0, :, None] + sv[:, :, 1, None, :]).reshape(C, PEER_HEADS, PEER_TOPK * PEER_TOPK)
        cidx = (si[:, :, 0, :, None] * N_KEYS + si[:, :, 1, None, :]).reshape(C, PEER_HEADS, PEER_TOPK * PEER_TOPK)
        top_s, pos = lax.top_k(cand, PEER_TOPK)
        idx = jnp.take_along_axis(cidx, pos, axis=-1)
        gate = jax.nn.softmax(top_s, axis=-1).astype(xc.dtype)
        u = expert_u[idx]
        a = jnp.einsum('cd,chkd->chk', xc, u)
        hid = jax.nn.gelu(a, approximate=False) * gate
        return jnp.einsum('chk,chkd->cd', hid, expert_v[idx])

    return lax.map(chunk, hc).reshape(B, T, D)


def encoder_layer(x, g_mix, w_in, pool_w, pool_scale, q_norm, k_norm, w_pool_out, w_attn_out, w_out,
                  g_ffn, w_query, sub_keys, expert_u, expert_v):
    B, T, _ = x.shape
    ROWS = T // GRID_W
    rows = jnp.repeat(jnp.arange(ROWS), GRID_W)
    cols = jnp.tile(jnp.arange(GRID_W), ROWS)

    h = rmsnorm(x, g_mix)
    z = h @ w_in
    o1 = POOL_WIDTH
    o2 = o1 + Q_WIDTH
    o3 = o2 + KV_WIDTH
    o4 = o3 + KV_WIDTH
    o5 = o4 + D_MODEL
    p = z[..., :o1]
    q = z[..., o1:o2].reshape(B, T, N_HEADS, HEAD_DIM)
    k = z[..., o2:o3].reshape(B, T, N_KV_HEADS, HEAD_DIM)
    v = z[..., o3:o4].reshape(B, T, N_KV_HEADS, HEAD_DIM)
    gate_pool = jax.nn.sigmoid(z[..., o4:o5])
    gate_attn = jax.nn.sigmoid(z[..., o5:])

    pool_branch = pool_mixer(p, pool_w, pool_scale) @ w_pool_out
    q = axial_rope(rmsnorm(q, q_norm), rows, cols)
    k = axial_rope(rmsnorm(k, k_norm), rows, cols)
    attn_branch = gqa_attention(q, k, v) @ w_attn_out

    merged = gate_pool * pool_branch + gate_attn * attn_branch
    x = x + merged @ w_out
    x = x + peer(rmsnorm(x, g_ffn), w_query, sub_keys, expert_u, expert_v)
    return x


def trunk(x, g_mix, w_in, pool_w, pool_scale, q_norm, k_norm, w_pool_out, w_attn_out, w_out,
          g_ffn, w_query, sub_keys, expert_u, expert_v, g_final):
    for l in range(DEPTH):
        x = encoder_layer(x, g_mix[l], w_in[l], pool_w[l], pool_scale[l], q_norm[l], k_norm[l],
                          w_pool_out[l], w_attn_out[l], w_out[l], g_ffn[l], w_query[l], sub_keys[l],
                          expert_u[l], expert_v[l])
    return rmsnorm(x, g_final)


def setup_inputs(seed: int = 0) -> dict:
    key = jax.random.key(seed)
    ks = jax.random.split(key, 17)
    f32 = jnp.float32
    nrm = lambda k, shape: jax.random.normal(k, shape, f32)
    return {
        "x_prompt": nrm(ks[0], (BATCH, SEQ, D_MODEL)),
        "x_sample": nrm(ks[1], (DEC_BATCH, DEC_SEQ, D_MODEL)),
        "g_mix": 1.0 + 0.05 * nrm(ks[2], (DEPTH, D_MODEL)),
        "w_in": nrm(ks[3], (DEPTH, D_MODEL, IN_WIDTH)) * D_MODEL ** -0.5,
        "pool_w": nrm(ks[4], (DEPTH, len(POOL_WINDOWS), POOL_GROUP, POOL_GROUP)) * POOL_GROUP ** -0.5,
        "pool_scale": 1.0 + 0.1 * nrm(ks[5], (DEPTH, POOL_WIDTH)),
        "q_norm": 1.0 + 0.05 * nrm(ks[6], (DEPTH, HEAD_DIM)),
        "k_norm": 1.0 + 0.05 * nrm(ks[7], (DEPTH, HEAD_DIM)),
        "w_pool_out": nrm(ks[8], (DEPTH, POOL_WIDTH, D_MODEL)) * POOL_WIDTH ** -0.5,
        "w_attn_out": nrm(ks[9], (DEPTH, Q_WIDTH, D_MODEL)) * Q_WIDTH ** -0.5,
        "w_out": nrm(ks[10], (DEPTH, D_MODEL, D_MODEL)) * D_MODEL ** -0.5,
        "g_ffn": 1.0 + 0.05 * nrm(ks[11], (DEPTH, D_MODEL)),
        "w_query": nrm(ks[12], (DEPTH, D_MODEL, PEER_HEADS * PEER_QDIM)) * D_MODEL ** -0.5,
        "sub_keys": nrm(ks[13], (DEPTH, PEER_HEADS, 2, N_KEYS, PEER_HALF)) * PEER_HALF ** -0.5,
        "expert_u": nrm(ks[14], (DEPTH, N_EXPERTS, D_MODEL)) * D_MODEL ** -0.5,
        "expert_v": nrm(ks[15], (DEPTH, N_EXPERTS, D_MODEL)) * 0.25,
        "g_final": 1.0 + 0.05 * nrm(ks[16], (D_MODEL,)),
    }


def reference(x_prompt, x_sample, g_mix, w_in, pool_w, pool_scale, q_norm, k_norm, w_pool_out, w_attn_out,
              w_out, g_ffn, w_query, sub_keys, expert_u, expert_v, g_final):
    y_prompt = trunk(x_prompt, g_mix, w_in, pool_w, pool_scale, q_norm, k_norm, w_pool_out, w_attn_out,
                     w_out, g_ffn, w_query, sub_keys, expert_u, expert_v, g_final)
    y_sample = trunk(x_sample, g_mix, w_in, pool_w, pool_scale, q_norm, k_norm, w_pool_out, w_attn_out,
                     w_out, g_ffn, w_query, sub_keys, expert_u, expert_v, g_final)
    return (y_prompt, y_sample)
```

```python
import functools
import math

import jax
import jax.numpy as jnp
from jax import lax
from jax.experimental import pallas as pl
from jax.experimental.pallas import tpu as pltpu

GRID_W = 64
POOL_WINDOWS = (2, 4, 8, 16)
POOL_GROUP = 128
N_HEADS = 8
N_KV_HEADS = 2
HEAD_DIM = 64
ROPE_THETA = 10000.0
N_KEYS = 128
PEER_HEADS = 8
PEER_HALF = 128
PEER_TOPK = 16
EPS = 1e-6

LANES = 128
SUBLANES = 8
VMEM_LIMIT_BYTES = 56 * 1024 * 1024

POOL_HALO = 16

F32 = jnp.float32
BF16 = jnp.bfloat16
NEG_INF = float("-inf")


def _cparams(n_axes):
    return pltpu.CompilerParams(
        dimension_semantics=("arbitrary",) * n_axes,
        vmem_limit_bytes=VMEM_LIMIT_BYTES,
    )


def _const_spec(shape):
    nd = len(shape)
    return pl.BlockSpec(shape, lambda *_: (0,) * nd)


def _in_proj_kernel(x_ref, g_ref, w_ref, qg_ref, qgp_ref, kg_ref, kgp_ref, cos_ref, sin_ref,
                    p_ref, q_ref, k0_ref, k1_ref, v0_ref, v1_ref, gp_ref, ga_ref, *, widths):
    x = x_ref[...]
    r = lax.rsqrt(jnp.mean(x * x, axis=-1, keepdims=True) + EPS)
    h = ((x * r) * g_ref[...]).astype(BF16)

    offs = {}
    o = 0
    for name, w in widths:
        offs[name] = (o, w)
        o += w

    def proj(name):
        a, w = offs[name]
        return jnp.dot(h, w_ref[:, a:a + w], preferred_element_type=F32)

    p_ref[...] = proj("p")

    cos = cos_ref[...]
    sin = sin_ref[...]
    lane = lax.broadcasted_iota(jnp.int32, (1, LANES), 1)
    first_head = lane < HEAD_DIM

    def norm_rope(z, zp, gain, gain_p):
        sq = z * z
        ss_a = jnp.sum(jnp.where(first_head, sq, 0.0), axis=-1, keepdims=True)
        ss_b = jnp.sum(jnp.where(first_head, 0.0, sq), axis=-1, keepdims=True)
        rr = lax.rsqrt(jnp.where(first_head, ss_a, ss_b) * (1.0 / HEAD_DIM) + EPS)
        return ((z * rr) * gain) * cos + ((zp * rr) * gain_p) * sin

    zq = proj("q")
    zqp = proj("qp")
    scale = HEAD_DIM ** -0.5
    for c in range(zq.shape[1] // LANES):
        sl = slice(c * LANES, (c + 1) * LANES)
        qr = norm_rope(zq[:, sl], zqp[:, sl], qg_ref[...], qgp_ref[...])
        q_ref[:, sl] = (qr * scale).astype(BF16)

    kr = norm_rope(proj("k"), proj("kp"), kg_ref[...], kgp_ref[...]).astype(BF16)
    k0_ref[...] = kr[:, :HEAD_DIM]
    k1_ref[...] = kr[:, HEAD_DIM:]
    zv = proj("v").astype(BF16)
    v0_ref[...] = zv[:, :HEAD_DIM]
    v1_ref[...] = zv[:, HEAD_DIM:]
    gp_ref[...] = jax.nn.sigmoid(proj("gp")).astype(BF16)
    ga_ref[...] = jax.nn.sigmoid(proj("ga")).astype(BF16)


def _rope_partner_perm(n_heads):
    q = HEAD_DIM // 4
    base = jnp.concatenate([jnp.arange(q, 2 * q), jnp.arange(0, q),
                            jnp.arange(3 * q, 4 * q), jnp.arange(2 * q, 3 * q)])
    return (jnp.arange(n_heads)[:, None] * HEAD_DIM + base[None, :]).reshape(-1)


def _rope_tables(seq):
    quarter = HEAD_DIM // 4
    inv_freq = ROPE_THETA ** (-jnp.arange(quarter, dtype=F32) / quarter)
    t = jnp.arange(seq)
    ang_r = (t // GRID_W).astype(F32)[:, None] * inv_freq
    ang_c = (t % GRID_W).astype(F32)[:, None] * inv_freq
    cos = jnp.concatenate([jnp.cos(ang_r)] * 2 + [jnp.cos(ang_c)] * 2, axis=-1)
    sin = jnp.concatenate([-jnp.sin(ang_r), jnp.sin(ang_r), -jnp.sin(ang_c), jnp.sin(ang_c)], axis=-1)
    return jnp.tile(cos, (1, 2)), jnp.tile(sin, (1, 2))


def _in_proj(x2d, seq, g_mix, w_in, q_norm, k_norm, *, tm):
    n, d = x2d.shape
    pool_w = POOL_GROUP * len(POOL_WINDOWS)
    q_w = N_HEADS * HEAD_DIM
    kv_w = N_KV_HEADS * HEAD_DIM
    o1 = pool_w
    o2 = o1 + q_w
    o3 = o2 + kv_w
    o4 = o3 + kv_w
    o5 = o4 + d
    wq = w_in[:, o1:o2]
    wk = w_in[:, o2:o3]
    w_cat = jnp.concatenate(
        [w_in[:, :o1], wq, wq[:, _rope_partner_perm(N_HEADS)], wk, wk[:, _rope_partner_perm(N_KV_HEADS)],
         w_in[:, o3:o4], w_in[:, o4:o5], w_in[:, o5:]], axis=1).astype(BF16)
    widths = (("p", pool_w), ("q", q_w), ("qp", q_w), ("k", kv_w), ("kp", kv_w), ("v", kv_w),
              ("gp", d), ("ga", d))
    perm1 = _rope_partner_perm(1)
    qg = jnp.tile(q_norm, 2)[None, :]
    qgp = jnp.tile(q_norm[perm1], 2)[None, :]
    kg = jnp.tile(k_norm, 2)[None, :]
    kgp = jnp.tile(k_norm[perm1], 2)[None, :]
    cos, sin = _rope_tables(seq)
    nb_seq = seq // tm

    row = lambda w: pl.BlockSpec((tm, w), lambda i: (i, 0))
    tab = pl.BlockSpec((tm, LANES), lambda i: (i % nb_seq, 0))
    out_shape = (
        jax.ShapeDtypeStruct((n, pool_w), F32),
        jax.ShapeDtypeStruct((n, q_w), BF16),
        jax.ShapeDtypeStruct((n, HEAD_DIM), BF16), jax.ShapeDtypeStruct((n, HEAD_DIM), BF16),
        jax.ShapeDtypeStruct((n, HEAD_DIM), BF16), jax.ShapeDtypeStruct((n, HEAD_DIM), BF16),
        jax.ShapeDtypeStruct((n, d), BF16), jax.ShapeDtypeStruct((n, d), BF16),
    )
    return pl.pallas_call(
        functools.partial(_in_proj_kernel, widths=widths),
        grid=(n // tm,),
        in_specs=[row(d), _const_spec((1, d)), _const_spec(w_cat.shape),
                  _const_spec((1, LANES)), _const_spec((1, LANES)), _const_spec((1, LANES)),
                  _const_spec((1, LANES)), tab, tab],
        out_specs=(row(pool_w), row(q_w), row(HEAD_DIM), row(HEAD_DIM), row(HEAD_DIM), row(HEAD_DIM),
                   row(d), row(d)),
        out_shape=out_shape,
        compiler_params=_cparams(1),
        name="in_proj",
    )(x2d, g_mix[None, :], w_cat, qg, qgp, kg, kgp, cos, sin)


def _attn_kernel(q_ref, k0_ref, k1_ref, v0_ref, v1_ref, o_ref):
    grp = N_HEADS // N_KV_HEADS
    q = q_ref[0]
    for g, (k_ref, v_ref) in enumerate(((k0_ref, v0_ref), (k1_ref, v1_ref))):
        qs = jnp.concatenate(
            [q[:, (g * grp + j) * HEAD_DIM:(g * grp + j + 1) * HEAD_DIM] for j in range(grp)], axis=0)
        s = lax.dot_general(qs, k_ref[0], (((1,), (1,)), ((), ())), preferred_element_type=F32)
        m = jnp.max(s, axis=-1, keepdims=True)
        e = jnp.exp(s - m)
        l = jnp.sum(e, axis=-1, keepdims=True)
        pr = (e / l).astype(BF16)
        o = jnp.dot(pr, v_ref[0], preferred_element_type=F32).astype(BF16)
        tq = q.shape[0]
        for j in range(grp):
            h = g * grp + j
            o_ref[0, :, h * HEAD_DIM:(h + 1) * HEAD_DIM] = o[j * tq:(j + 1) * tq]


def _attention(q, k0, k1, v0, v1, *, tq):
    b, t, qw = q.shape
    kv = pl.BlockSpec((1, t, HEAD_DIM), lambda bi, i: (bi, 0, 0))
    qs = pl.BlockSpec((1, tq, qw), lambda bi, i: (bi, i, 0))
    return pl.pallas_call(
        _attn_kernel,
        grid=(b, t // tq),
        in_specs=[qs, kv, kv, kv, kv],
        out_specs=qs,
        out_shape=jax.ShapeDtypeStruct((b, t, qw), BF16),
        compiler_params=_cparams(2),
        name="attention",
    )(q, k0, k1, v0, v1)


def _merge_kernel(x_ref, p_ref, prev_ref, next_ref, attn_ref, gp_ref, ga_ref, pw_ref, ps_ref,
                  wpo_ref, wao_ref, wo_ref, o_ref, e_ref, *, nb_seq, seq):
    i = pl.program_id(0)
    tm = x_ref.shape[0]
    blk = i % nb_seq
    halo = POOL_HALO
    e_ref[0:halo, :] = jnp.where(blk == 0, 0.0, prev_ref[...])
    e_ref[halo:halo + tm, :] = p_ref[...]
    e_ref[halo + tm:, :] = jnp.where(blk == nb_seq - 1, 0.0, next_ref[...])

    t = blk * tm + lax.broadcasted_iota(jnp.int32, (tm, 1), 0)
    mixed = []
    for gi, w in enumerate(POOL_WINDOWS):
        cols = slice(gi * POOL_GROUP, (gi + 1) * POOL_GROUP)
        acc = None
        for dlt in range(-(w // 2), w - w // 2):
            piece = e_ref[halo + dlt:halo + dlt + tm, cols]
            acc = piece if acc is None else acc + piece
        cnt = (jnp.minimum(t + (w - w // 2), seq) - jnp.maximum(t - w // 2, 0)).astype(F32)
        pooled = (acc / cnt - e_ref[halo:halo + tm, cols]).astype(BF16)
        mixed.append(jnp.dot(pooled, pw_ref[gi], preferred_element_type=F32) * ps_ref[:, cols])
    mixed = jnp.concatenate(mixed, axis=-1).astype(BF16)
    pool_branch = jnp.dot(mixed, wpo_ref[...], preferred_element_type=F32)
    attn_branch = jnp.dot(attn_ref[...], wao_ref[...], preferred_element_type=F32)
    merged = gp_ref[...].astype(F32) * pool_branch + ga_ref[...].astype(F32) * attn_branch
    o_ref[...] = x_ref[...] + jnp.dot(merged.astype(BF16), wo_ref[...], preferred_element_type=F32)


def _merge(x2d, seq, p, attn, gp, ga, pool_w, pool_scale, w_pool_out, w_attn_out, w_out, *, tm):
    n, d = x2d.shape
    pw = p.shape[1]
    nb_seq = seq // tm
    hb = tm // POOL_HALO
    n_halo_blocks = n // POOL_HALO
    row = lambda w: pl.BlockSpec((tm, w), lambda i: (i, 0))
    prev = pl.BlockSpec((POOL_HALO, pw), lambda i: (jnp.maximum(i * hb - 1, 0), 0))
    nxt = pl.BlockSpec((POOL_HALO, pw), lambda i: (jnp.minimum((i + 1) * hb, n_halo_blocks - 1), 0))
    return pl.pallas_call(
        functools.partial(_merge_kernel, nb_seq=nb_seq, seq=seq),
        grid=(n // tm,),
        in_specs=[row(d), row(pw), prev, nxt, row(attn.shape[1]), row(d), row(d),
                  _const_spec(pool_w.shape), _const_spec((1, pw)), _const_spec(w_pool_out.shape),
                  _const_spec(w_attn_out.shape), _const_spec(w_out.shape)],
        out_specs=row(d),
        out_shape=jax.ShapeDtypeStruct((n, d), F32),
        scratch_shapes=[pltpu.VMEM((tm + 2 * POOL_HALO, pw), F32)],
        compiler_params=_cparams(1),
        name="merge",
    )(x2d, p, p, p, attn, gp, ga, pool_w.astype(BF16), pool_scale[None, :], w_pool_out.astype(BF16),
      w_attn_out.astype(BF16), w_out.astype(BF16))


def _topk_rows(vals, payload, k):
    n = vals.shape[0]
    iota = lax.broadcasted_iota(jnp.int32, vals.shape, 0)
    cur = vals
    out_v, out_p = [], []
    for _ in range(k):
        m = jnp.max(cur, axis=0, keepdims=True)
        pos = jnp.min(jnp.where(cur == m, iota, n), axis=0, keepdims=True)
        hit = iota == pos
        out_v.append(m)
        out_p.append(jnp.sum(jnp.where(hit, payload, 0), axis=0, keepdims=True))
        cur = jnp.where(hit, NEG_INF, cur)
    return jnp.concatenate(out_v, axis=0), jnp.concatenate(out_p, axis=0)


def _peer_route_kernel(x_ref, g_ref, wq_ref, keys_ref, idx_ref, gate_ref):
    x = x_ref[...]
    r = lax.rsqrt(jnp.mean(x * x, axis=-1, keepdims=True) + EPS)
    h = ((x * r) * g_ref[...]).astype(BF16)
    tb = x.shape[0]
    key_iota = lax.broadcasted_iota(jnp.int32, (N_KEYS, tb), 0)
    idx_rows, gate_rows = [], []
    for hd in range(PEER_HEADS):
        sv, si = [], []
        for part in range(2):
            c0 = (hd * 2 + part) * PEER_HALF
            qh = jnp.dot(h, wq_ref[:, c0:c0 + PEER_HALF], preferred_element_type=F32).astype(BF16)
            s_t = lax.dot_general(keys_ref[hd * 2 + part], qh, (((1,), (1,)), ((), ())),
                                  preferred_element_type=F32)
            v, ix = _topk_rows(s_t, key_iota, PEER_TOPK)
            sv.append(v)
            si.append(ix)
        cand = jnp.concatenate([sv[0][a:a + 1] + sv[1] for a in range(PEER_TOPK)], axis=0)
        cidx = jnp.concatenate([si[0][a:a + 1] * N_KEYS + si[1] for a in range(PEER_TOPK)], axis=0)
        top_s, idx = _topk_rows(cand, cidx, PEER_TOPK)
        e = jnp.exp(top_s - top_s[0:1])
        gate_rows.append(e / jnp.sum(e, axis=0, keepdims=True))
        idx_rows.append(idx)
    idx_ref[...] = jnp.concatenate(idx_rows, axis=0).T
    gate_ref[...] = jnp.concatenate(gate_rows, axis=0).T


def _peer_route(x1, g_ffn, w_query, sub_keys, *, tb):
    n, d = x1.shape
    nk = PEER_HEADS * PEER_TOPK
    keys = sub_keys.reshape(PEER_HEADS * 2, N_KEYS, PEER_HALF).astype(BF16)
    row = lambda w: pl.BlockSpec((tb, w), lambda i: (i, 0))
    return pl.pallas_call(
        _peer_route_kernel,
        grid=(n // tb,),
        in_specs=[row(d), _const_spec((1, d)), _const_spec(w_query.shape), _const_spec(keys.shape)],
        out_specs=(row(nk), row(nk)),
        out_shape=(jax.ShapeDtypeStruct((n, nk), jnp.int32), jax.ShapeDtypeStruct((n, nk), F32)),
        compiler_params=_cparams(1),
        name="peer_route",
    )(x1, g_ffn[None, :], w_query.astype(BF16), keys)


ROW_WORDS = 4


def _pack_table(tbl):
    e, d = tbl.shape
    bits = lax.bitcast_convert_type(tbl.astype(BF16), jnp.uint16).astype(jnp.uint32)
    half = d // 2
    words = bits[:, :half] | (bits[:, half:] << 16)
    return lax.bitcast_convert_type(words, jnp.int32).reshape(e * ROW_WORDS, LANES)


def _unpack(w):
    lo = lax.bitcast_convert_type(w << 16, F32)
    hi = lax.bitcast_convert_type(w & jnp.int32(-65536), F32)
    return lo, hi


def _load_table(tbl_hbm, tbl_vmem, sem):
    @pl.when(pl.program_id(0) == 0)
    def _():
        cp = pltpu.make_async_copy(tbl_hbm, tbl_vmem, sem)
        cp.start()
        cp.wait()


def _peer_up_kernel(idx_ref, x_ref, g_ref, gate_ref, tbl_hbm, hid_ref, tbl, sem, h_s, s_s, a_s):
    _load_table(tbl_hbm, tbl, sem)
    x = x_ref[...]
    ss = jnp.sum(jnp.sum(x * x, axis=2, keepdims=True), axis=1, keepdims=True)
    h_s[...] = (x * lax.rsqrt(ss * (1.0 / (SUBLANES * LANES)) + EPS)) * g_ref[...]
    tb = x.shape[0]
    nk = idx_ref.shape[1]
    ones = jnp.ones((SUBLANES, LANES), F32)

    def token(c, carry):
        xv = h_s[c]
        xl = xv[:ROW_WORDS]
        xh = xv[ROW_WORDS:]
        for r in range(nk):
            row = pl.multiple_of(idx_ref[c, r] * ROW_WORDS, ROW_WORDS)
            lo, hi = _unpack(tbl[pl.ds(row, ROW_WORDS), :])
            s_s[r * ROW_WORDS:(r + 1) * ROW_WORDS, :] = lo * xl + hi * xh
        part = s_s[pl.ds(0, nk, stride=ROW_WORDS), :]
        for j in range(1, ROW_WORDS):
            part = part + s_s[pl.ds(j, nk, stride=ROW_WORDS), :]
        a_row = lax.dot_general(ones, part, (((1,), (1,)), ((), ())), precision=lax.Precision.HIGHEST,
                                preferred_element_type=F32)
        a_s[pl.ds(c, 1), :] = a_row[0:1]
        return carry

    lax.fori_loop(0, tb, token, 0)
    a = a_s[...]
    hid_ref[...] = (0.5 * a * (1.0 + lax.erf(a * (2.0 ** -0.5)))) * gate_ref[...]


def _peer_up(x1r, g_ffn, idx, gate, tbl_u, *, tb):
    n = x1r.shape[0]
    nk = idx.shape[1]
    return pl.pallas_call(
        _peer_up_kernel,
        grid=(n // tb,),
        in_specs=[pl.BlockSpec((tb, nk), lambda i: (i, 0), memory_space=pltpu.SMEM),
                  pl.BlockSpec((tb, SUBLANES, LANES), lambda i: (i, 0, 0)),
                  _const_spec((SUBLANES, LANES)),
                  pl.BlockSpec((tb, nk), lambda i: (i, 0)),
                  pl.BlockSpec(memory_space=pl.ANY)],
        out_specs=pl.BlockSpec((tb, nk), lambda i: (i, 0)),
        out_shape=jax.ShapeDtypeStruct((n, nk), F32),
        scratch_shapes=[pltpu.VMEM(tbl_u.shape, jnp.int32), pltpu.SemaphoreType.DMA,
                        pltpu.VMEM((tb, SUBLANES, LANES), F32),
                        pltpu.VMEM((nk * ROW_WORDS, LANES), F32),
                        pltpu.VMEM((tb, nk), F32)],
        compiler_params=_cparams(1),
        name="peer_up",
    )(idx, x1r, g_ffn.reshape(SUBLANES, LANES), gate, tbl_u)


def _peer_down_kernel(idx_ref, hid_ref, x_ref, g_ref, tbl_hbm, y_ref, tbl, sem, o_s):
    _load_table(tbl_hbm, tbl, sem)
    tb = x_ref.shape[0]
    nk = idx_ref.shape[1]
    n_acc = 4

    def token(c, carry):
        acc_lo = [jnp.zeros((ROW_WORDS, LANES), F32) for _ in range(n_acc)]
        acc_hi = [jnp.zeros((ROW_WORDS, LANES), F32) for _ in range(n_acc)]
        for r in range(nk):
            row = pl.multiple_of(idx_ref[c, r] * ROW_WORDS, ROW_WORDS)
            lo, hi = _unpack(tbl[pl.ds(row, ROW_WORDS), :])
            wgt = hid_ref[c, r]
            acc_lo[r % n_acc] = acc_lo[r % n_acc] + wgt * lo
            acc_hi[r % n_acc] = acc_hi[r % n_acc] + wgt * hi
        lo = (acc_lo[0] + acc_lo[1]) + (acc_lo[2] + acc_lo[3])
        hi = (acc_hi[0] + acc_hi[1]) + (acc_hi[2] + acc_hi[3])
        o_s[c] = x_ref[c] + jnp.concatenate([lo, hi], axis=0)
        return carry

    lax.fori_loop(0, tb, token, 0)
    xo = o_s[...]
    ss = jnp.sum(jnp.sum(xo * xo, axis=2, keepdims=True), axis=1, keepdims=True)
    y_ref[...] = (xo * lax.rsqrt(ss * (1.0 / (SUBLANES * LANES)) + EPS)) * g_ref[...]


def _peer_down(x1r, g_final, idx, hid, tbl_v, *, tb):
    n = x1r.shape[0]
    nk = idx.shape[1]
    smem = pl.BlockSpec((tb, nk), lambda i: (i, 0), memory_space=pltpu.SMEM)
    tok = pl.BlockSpec((tb, SUBLANES, LANES), lambda i: (i, 0, 0))
    return pl.pallas_call(
        _peer_down_kernel,
        grid=(n // tb,),
        in_specs=[smem, smem, tok, _const_spec((SUBLANES, LANES)), pl.BlockSpec(memory_space=pl.ANY)],
        out_specs=tok,
        out_shape=jax.ShapeDtypeStruct((n, SUBLANES, LANES), F32),
        scratch_shapes=[pltpu.VMEM(tbl_v.shape, jnp.int32), pltpu.SemaphoreType.DMA,
                        pltpu.VMEM((tb, SUBLANES, LANES), F32)],
        compiler_params=_cparams(1),
        name="peer_down",
    )(idx, hid, x1r, g_final.reshape(SUBLANES, LANES), tbl_v)


def _pick(n, pref):
    t = min(pref, n)
    while n % t:
        t //= 2
    return t


def _trunk(x, g_mix, w_in, pool_w, pool_scale, q_norm, k_norm, w_pool_out, w_attn_out, w_out,
           g_ffn, w_query, sub_keys, tbl_u, tbl_v, g_final):
    b, t, d = x.shape
    n = b * t
    x2d = x.reshape(n, d)
    tm = _pick(t, 512)
    p, q, k0, k1, v0, v1, gp, ga = _in_proj(x2d, t, g_mix, w_in, q_norm, k_norm, tm=tm)
    r3 = lambda a: a.reshape(b, t, a.shape[-1])
    attn = _attention(r3(q), r3(k0), r3(k1), r3(v0), r3(v1), tq=_pick(t, 128))
    x1 = _merge(x2d, t, p, attn.reshape(n, -1), gp, ga, pool_w, pool_scale, w_pool_out, w_attn_out,
                w_out, tm=tm)
    idx, gate = _peer_route(x1, g_ffn, w_query, sub_keys, tb=_pick(n, 256))
    x1r = x1.reshape(n, SUBLANES, LANES)
    tbp = _pick(n, 64)
    hid = _peer_up(x1r, g_ffn, idx, gate, tbl_u, tb=tbp)
    y = _peer_down(x1r, g_final, idx, hid, tbl_v, tb=tbp)
    return y.reshape(b, t, d)


def kernel(x_prompt, x_sample, g_mix, w_in, pool_w, pool_scale, q_norm, k_norm, w_pool_out, w_attn_out,
           w_out, g_ffn, w_query, sub_keys, expert_u, expert_v, g_final):
    assert g_mix.shape[0] == 1, "single-layer trunk"
    tbl_u = _pack_table(expert_u[0])
    tbl_v = _pack_table(expert_v[0])
    args = (g_mix[0], w_in[0], pool_w[0], pool_scale[0], q_norm[0], k_norm[0], w_pool_out[0],
            w_attn_out[0], w_out[0], g_ffn[0], w_query[0], sub_keys[0], tbl_u, tbl_v, g_final)
    return (_trunk(x_prompt, *args), _trunk(x_sample, *args))
```

```python
import functools
import math

import jax
import jax.numpy as jnp
from jax import lax
from jax.experimental import pallas as pl
from jax.experimental.pallas import tpu as pltpu

GRID_W = 64
POOL_WINDOWS = (2, 4, 8, 16)
POOL_GROUP = 128
N_HEADS = 8
N_KV_HEADS = 2
HEAD_DIM = 64
ROPE_THETA = 10000.0
N_KEYS = 128
PEER_HEADS = 8
PEER_HALF = 128
PEER_TOPK = 16
EPS = 1e-6

LANES = 128
SUBLANES = 8
VMEM_LIMIT_BYTES = 56 * 1024 * 1024

POOL_HALO = 16

F32 = jnp.float32
BF16 = jnp.bfloat16
NEG_INF = float("-inf")


def _cparams(n_axes):
    return pltpu.CompilerParams(
        dimension_semantics=("arbitrary",) * n_axes,
        vmem_limit_bytes=VMEM_LIMIT_BYTES,
    )


def _const_spec(shape):
    nd = len(shape)
    return pl.BlockSpec(shape, lambda *_: (0,) * nd)


def _in_proj_kernel(x_ref, g_ref, w_ref, qg_ref, qgp_ref, kg_ref, kgp_ref, cos_ref, sin_ref,
                    p_ref, q_ref, k0_ref, k1_ref, v0_ref, v1_ref, gp_ref, ga_ref, *, widths):
    x = x_ref[...]
    r = lax.rsqrt(jnp.mean(x * x, axis=-1, keepdims=True) + EPS)
    h = ((x * r) * g_ref[...]).astype(BF16)

    offs = {}
    o = 0
    for name, w in widths:
        offs[name] = (o, w)
        o += w

    def proj(name):
        a, w = offs[name]
        return jnp.dot(h, w_ref[:, a:a + w], preferred_element_type=F32)

    p_ref[...] = proj("p")

    cos = cos_ref[...]
    sin = sin_ref[...]
    lane = lax.broadcasted_iota(jnp.int32, (1, LANES), 1)
    first_head = lane < HEAD_DIM

    def norm_rope(z, zp, gain, gain_p):
        sq = z * z
        ss_a = jnp.sum(jnp.where(first_head, sq, 0.0), axis=-1, keepdims=True)
        ss_b = jnp.sum(jnp.where(first_head, 0.0, sq), axis=-1, keepdims=True)
        rr = lax.rsqrt(jnp.where(first_head, ss_a, ss_b) * (1.0 / HEAD_DIM) + EPS)
        return ((z * rr) * gain) * cos + ((zp * rr) * gain_p) * sin

    zq = proj("q")
    zqp = proj("qp")
    scale = HEAD_DIM ** -0.5
    for c in range(zq.shape[1] // LANES):
        sl = slice(c * LANES, (c + 1) * LANES)
        qr = norm_rope(zq[:, sl], zqp[:, sl], qg_ref[...], qgp_ref[...])
        q_ref[:, sl] = (qr * scale).astype(BF16)

    kr = norm_rope(proj("k"), proj("kp"), kg_ref[...], kgp_ref[...]).astype(BF16)
    k0_ref[...] = kr[:, :HEAD_DIM]
    k1_ref[...] = kr[:, HEAD_DIM:]
    zv = proj("v").astype(BF16)
    v0_ref[...] = zv[:, :HEAD_DIM]
    v1_ref[...] = zv[:, HEAD_DIM:]
    gp_ref[...] = jax.nn.sigmoid(proj("gp")).astype(BF16)
    ga_ref[...] = jax.nn.sigmoid(proj("ga")).astype(BF16)


def _rope_partner_perm(n_heads):
    q = HEAD_DIM // 4
    base = jnp.concatenate([jnp.arange(q, 2 * q), jnp.arange(0, q),
                            jnp.arange(3 * q, 4 * q), jnp.arange(2 * q, 3 * q)])
    return (jnp.arange(n_heads)[:, None] * HEAD_DIM + base[None, :]).reshape(-1)


def _rope_tables(seq):
    quarter = HEAD_DIM // 4
    inv_freq = ROPE_THETA ** (-jnp.arange(quarter, dtype=F32) / quarter)
    t = jnp.arange(seq)
    ang_r = (t // GRID_W).astype(F32)[:, None] * inv_freq
    ang_c = (t % GRID_W).astype(F32)[:, None] * inv_freq
    cos = jnp.concatenate([jnp.cos(ang_r)] * 2 + [jnp.cos(ang_c)] * 2, axis=-1)
    sin = jnp.concatenate([-jnp.sin(ang_r), jnp.sin(ang_r), -jnp.sin(ang_c), jnp.sin(ang_c)], axis=-1)
    return jnp.tile(cos, (1, 2)), jnp.tile(sin, (1, 2))


def _in_proj(x2d, seq, g_mix, w_in, q_norm, k_norm, *, tm):
    n, d = x2d.shape
    pool_w = POOL_GROUP * len(POOL_WINDOWS)
    q_w = N_HEADS * HEAD_DIM
    kv_w = N_KV_HEADS * HEAD_DIM
    o1 = pool_w
    o2 = o1 + q_w
    o3 = o2 + kv_w
    o4 = o3 + kv_w
    o5 = o4 + d
    wq = w_in[:, o1:o2]
    wk = w_in[:, o2:o3]
    w_cat = jnp.concatenate(
        [w_in[:, :o1], wq, wq[:, _rope_partner_perm(N_HEADS)], wk, wk[:, _rope_partner_perm(N_KV_HEADS)],
         w_in[:, o3:o4], w_in[:, o4:o5], w_in[:, o5:]], axis=1).astype(BF16)
    widths = (("p", pool_w), ("q", q_w), ("qp", q_w), ("k", kv_w), ("kp", kv_w), ("v", kv_w),
              ("gp", d), ("ga", d))
    perm1 = _rope_partner_perm(1)
    qg = jnp.tile(q_norm, 2)[None, :]
    qgp = jnp.tile(q_norm[perm1], 2)[None, :]
    kg = jnp.tile(k_norm, 2)[None, :]
    kgp = jnp.tile(k_norm[perm1], 2)[None, :]
    cos, sin = _rope_tables(seq)
    nb_seq = seq // tm

    row = lambda w: pl.BlockSpec((tm, w), lambda i: (i, 0))
    tab = pl.BlockSpec((tm, LANES), lambda i: (i % nb_seq, 0))
    out_shape = (
        jax.ShapeDtypeStruct((n, pool_w), F32),
        jax.ShapeDtypeStruct((n, q_w), BF16),
        jax.ShapeDtypeStruct((n, HEAD_DIM), BF16), jax.ShapeDtypeStruct((n, HEAD_DIM), BF16),
        jax.ShapeDtypeStruct((n, HEAD_DIM), BF16), jax.ShapeDtypeStruct((n, HEAD_DIM), BF16),
        jax.ShapeDtypeStruct((n, d), BF16), jax.ShapeDtypeStruct((n, d), BF16),
    )
    return pl.pallas_call(
        functools.partial(_in_proj_kernel, widths=widths),
        grid=(n // tm,),
        in_specs=[row(d), _const_spec((1, d)), _const_spec(w_cat.shape),
                  _const_spec((1, LANES)), _const_spec((1, LANES)), _const_spec((1, LANES)),
                  _const_spec((1, LANES)), tab, tab],
        out_specs=(row(pool_w), row(q_w), row(HEAD_DIM), row(HEAD_DIM), row(HEAD_DIM), row(HEAD_DIM),
                   row(d), row(d)),
        out_shape=out_shape,
        compiler_params=_cparams(1),
        name="in_proj",
    )(x2d, g_mix[None, :], w_cat, qg, qgp, kg, kgp, cos, sin)


def _attn_kernel(q_ref, k0_ref, k1_ref, v0_ref, v1_ref, o_ref):
    grp = N_HEADS // N_KV_HEADS
    q = q_ref[0]
    for g, (k_ref, v_ref) in enumerate(((k0_ref, v0_ref), (k1_ref, v1_ref))):
        qs = jnp.concatenate(
            [q[:, (g * grp + j) * HEAD_DIM:(g * grp + j + 1) * HEAD_DIM] for j in range(grp)], axis=0)
        s = lax.dot_general(qs, k_ref[0], (((1,), (1,)), ((), ())), preferred_element_type=F32)
        m = jnp.max(s, axis=-1, keepdims=True)
        e = jnp.exp(s - m)
        l = jnp.sum(e, axis=-1, keepdims=True)
        pr = (e / l).astype(BF16)
        o = jnp.dot(pr, v_ref[0], preferred_element_type=F32).astype(BF16)
        tq = q.shape[0]
        for j in range(grp):
            h = g * grp + j
            o_ref[0, :, h * HEAD_DIM:(h + 1) * HEAD_DIM] = o[j * tq:(j + 1) * tq]


def _attention(q, k0, k1, v0, v1, *, tq):
    b, t, qw = q.shape
    kv = pl.BlockSpec((1, t, HEAD_DIM), lambda bi, i: (bi, 0, 0))
    qs = pl.BlockSpec((1, tq, qw), lambda bi, i: (bi, i, 0))
    return pl.pallas_call(
        _attn_kernel,
        grid=(b, t // tq),
        in_specs=[qs, kv, kv, kv, kv],
        out_specs=qs,
        out_shape=jax.ShapeDtypeStruct((b, t, qw), BF16),
        compiler_params=_cparams(2),
        name="attention",
    )(q, k0, k1, v0, v1)


def _merge_kernel(x_ref, p_ref, prev_ref, next_ref, attn_ref, gp_ref, ga_ref, pw_ref, ps_ref,
                  wpo_ref, wao_ref, wo_ref, o_ref, e_ref, *, nb_seq, seq):
    i = pl.program_id(0)
    tm = x_ref.shape[0]
    blk = i % nb_seq
    halo = POOL_HALO
    e_ref[0:halo, :] = jnp.where(blk == 0, 0.0, prev_ref[...])
    e_ref[halo:halo + tm, :] = p_ref[...]
    e_ref[halo + tm:, :] = jnp.where(blk == nb_seq - 1, 0.0, next_ref[...])

    t = blk * tm + lax.broadcasted_iota(jnp.int32, (tm, 1), 0)
    mixed = []
    for gi, w in enumerate(POOL_WINDOWS):
        cols = slice(gi * POOL_GROUP, (gi + 1) * POOL_GROUP)
        acc = None
        for dlt in range(-(w // 2), w - w // 2):
            piece = e_ref[halo + dlt:halo + dlt + tm, cols]
            acc = piece if acc is None else acc + piece
        cnt = (jnp.minimum(t + (w - w // 2), seq) - jnp.maximum(t - w // 2, 0)).astype(F32)
        pooled = (acc / cnt - e_ref[halo:halo + tm, cols]).astype(BF16)
        mixed.append(jnp.dot(pooled, pw_ref[gi], preferred_element_type=F32) * ps_ref[:, cols])
    mixed = jnp.concatenate(mixed, axis=-1).astype(BF16)
    pool_branch = jnp.dot(mixed, wpo_ref[...], preferred_element_type=F32)
    attn_branch = jnp.dot(attn_ref[...], wao_ref[...], preferred_element_type=F32)
    merged = gp_ref[...].astype(F32) * pool_branch + ga_ref[...].astype(F32) * attn_branch
    o_ref[...] = x_ref[...] + jnp.dot(merged.astype(BF16), wo_ref[...], preferred_element_type=F32)


def _merge(x2d, seq, p, attn, gp, ga, pool_w, pool_scale, w_pool_out, w_attn_out, w_out, *, tm):
    n, d = x2d.shape
    pw = p.shape[1]
    nb_seq = seq // tm
    hb = tm // POOL_HALO
    n_halo_blocks = n // POOL_HALO
    row = lambda w: pl.BlockSpec((tm, w), lambda i: (i, 0))
    prev = pl.BlockSpec((POOL_HALO, pw), lambda i: (jnp.maximum(i * hb - 1, 0), 0))
    nxt = pl.BlockSpec((POOL_HALO, pw), lambda i: (jnp.minimum((i + 1) * hb, n_halo_blocks - 1), 0))
    return pl.pallas_call(
        functools.partial(_merge_kernel, nb_seq=nb_seq, seq=seq),
        grid=(n // tm,),
        in_specs=[row(d), row(pw), prev, nxt, row(attn.shape[1]), row(d), row(d),
                  _const_spec(pool_w.shape), _const_spec((1, pw)), _const_spec(w_pool_out.shape),
                  _const_spec(w_attn_out.shape), _const_spec(w_out.shape)],
        out_specs=row(d),
        out_shape=jax.ShapeDtypeStruct((n, d), F32),
        scratch_shapes=[pltpu.VMEM((tm + 2 * POOL_HALO, pw), F32)],
        compiler_params=_cparams(1),
        name="merge",
    )(x2d, p, p, p, attn, gp, ga, pool_w.astype(BF16), pool_scale[None, :], w_pool_out.astype(BF16),
      w_attn_out.astype(BF16), w_out.astype(BF16))


def _topk_rows(vals, k, payload=None):
    n = vals.shape[0]
    iota = lax.broadcasted_iota(jnp.int32, vals.shape, 0).astype(F32)
    cur = vals
    out_v, out_p = [], []
    for _ in range(k):
        m = jnp.max(cur, axis=0, keepdims=True)
        pos = jnp.min(jnp.where(cur == m, iota, float(n)), axis=0, keepdims=True)
        hit = iota == pos
        out_v.append(m)
        if payload is None:
            out_p.append(pos)
        else:
            out_p.append(jnp.sum(jnp.where(hit, payload, 0.0), axis=0, keepdims=True))
        cur = jnp.where(hit, NEG_INF, cur)
    return jnp.concatenate(out_v, axis=0), jnp.concatenate(out_p, axis=0)


def _candidate_pairs():
    return [(a, b) for a in range(PEER_TOPK) for b in range(PEER_TOPK) if (a + 1) * (b + 1) <= PEER_TOPK]


def _pair_selectors():
    pairs = _candidate_pairs()
    n = len(pairs) + (-len(pairs) % SUBLANES)
    rows = jnp.arange(len(pairs))
    sel_a = jnp.zeros((n, PEER_TOPK), F32).at[rows, jnp.array([p[0] for p in pairs])].set(1.0)
    sel_b = jnp.zeros((n, PEER_TOPK), F32).at[rows, jnp.array([p[1] for p in pairs])].set(1.0)
    pad = jnp.where(jnp.arange(n) < len(pairs), 0.0, NEG_INF).astype(F32)[:, None]
    return sel_a, sel_b, pad


def _select_rows(sel, x):
    return jnp.dot(sel, x, precision=lax.Precision.HIGHEST, preferred_element_type=F32)


def _peer_route_kernel(x_ref, g_ref, wq_ref, keys_ref, sela_ref, selb_ref, pad_ref, rows_ref, gate_ref):
    x = x_ref[...]
    r = lax.rsqrt(jnp.mean(x * x, axis=-1, keepdims=True) + EPS)
    h = ((x * r) * g_ref[...]).astype(BF16)
    sel_a = sela_ref[...]
    sel_b = selb_ref[...]
    row_list, gate_rows = [], []
    for hd in range(PEER_HEADS):
        sv, si = [], []
        for part in range(2):
            c0 = (hd * 2 + part) * PEER_HALF
            qh = jnp.dot(h, wq_ref[:, c0:c0 + PEER_HALF], preferred_element_type=F32).astype(BF16)
            s_t = lax.dot_general(keys_ref[hd * 2 + part], qh, (((1,), (1,)), ((), ())),
                                  preferred_element_type=F32)
            v, ix = _topk_rows(s_t, PEER_TOPK)
            sv.append(v)
            si.append(ix)
        cand = (_select_rows(sel_a, sv[0]) + _select_rows(sel_b, sv[1])) + pad_ref[...]
        cidx = _select_rows(sel_a, si[0]) * float(N_KEYS) + _select_rows(sel_b, si[1])
        top_s, idx = _topk_rows(cand, PEER_TOPK, payload=cidx)
        e = jnp.exp(top_s - top_s[0:1])
        gate_rows.append(e / jnp.sum(e, axis=0, keepdims=True))
        row_list.append(idx)
    rows_ref[...] = (jnp.concatenate(row_list, axis=0).astype(jnp.int32) * ROW_WORDS).T
    gate_ref[...] = jnp.concatenate(gate_rows, axis=0).T


def _peer_route(x1, g_ffn, w_query, sub_keys, *, tb):
    n, d = x1.shape
    nk = PEER_HEADS * PEER_TOPK
    keys = sub_keys.reshape(PEER_HEADS * 2, N_KEYS, PEER_HALF).astype(BF16)
    sel_a, sel_b, pad = _pair_selectors()
    row = lambda w: pl.BlockSpec((tb, w), lambda i: (i, 0))
    return pl.pallas_call(
        _peer_route_kernel,
        grid=(n // tb,),
        in_specs=[row(d), _const_spec((1, d)), _const_spec(w_query.shape), _const_spec(keys.shape),
                  _const_spec(sel_a.shape), _const_spec(sel_b.shape), _const_spec(pad.shape)],
        out_specs=(row(nk), row(nk)),
        out_shape=(jax.ShapeDtypeStruct((n, nk), jnp.int32), jax.ShapeDtypeStruct((n, nk), F32)),
        compiler_params=_cparams(1),
        name="peer_route",
    )(x1, g_ffn[None, :], w_query.astype(BF16), keys, sel_a, sel_b, pad)


ROW_WORDS = 4


def _pack_table(tbl):
    e, d = tbl.shape
    bits = lax.bitcast_convert_type(tbl.astype(BF16), jnp.uint16).astype(jnp.uint32)
    half = d // 2
    words = bits[:, :half] | (bits[:, half:] << 16)
    return lax.bitcast_convert_type(words, jnp.int32).reshape(e * ROW_WORDS, LANES)


def _unpack(w):
    lo = lax.bitcast_convert_type(w << 16, F32)
    hi = lax.bitcast_convert_type(w & jnp.int32(-65536), F32)
    return lo, hi


def _resident_spec(shape):
    nd = len(shape)
    return pl.BlockSpec(shape, lambda *_: (0,) * nd, pipeline_mode=pl.Buffered(1))


def _smem_rows_spec(tb, nk):
    return pl.BlockSpec((tb, nk), lambda i: (i, 0), memory_space=pltpu.SMEM)


def _peer_up_kernel(rows_ref, x_ref, g_ref, gate_ref, ones_ref, eye_ref, tbl, hid_ref, h_s, s_s, a_s):
    x = x_ref[...]
    ss = jnp.sum(jnp.sum(x * x, axis=2, keepdims=True), axis=1, keepdims=True)
    h_s[...] = (x * lax.rsqrt(ss * (1.0 / (SUBLANES * LANES)) + EPS)) * g_ref[...]
    tb = x.shape[0]
    nk = rows_ref.shape[1]

    def token(c, carry):
        xv = h_s[c]
        xl = xv[:ROW_WORDS]
        xh = xv[ROW_WORDS:]
        for r in range(nk):
            row = pl.multiple_of(rows_ref[c, r], ROW_WORDS)
            lo, hi = _unpack(tbl[pl.ds(row, ROW_WORDS), :])
            s_s[r * ROW_WORDS:(r + 1) * ROW_WORDS, :] = lo * xl + hi * xh
        part = s_s[pl.ds(0, nk, stride=ROW_WORDS), :]
        for j in range(1, ROW_WORDS):
            part = part + s_s[pl.ds(j, nk, stride=ROW_WORDS), :]
        a_s[c] = part
        return carry

    lax.fori_loop(0, tb, token, 0)
    ones = ones_ref[...]
    tc = min(tb, 16)
    for c0 in range(0, tb, tc):
        part = a_s[c0:c0 + tc].reshape(tc * nk, LANES)
        p_hi = part.astype(BF16)
        p_lo = (part - p_hi.astype(F32)).astype(BF16)
        sums = (jnp.dot(p_hi, ones, preferred_element_type=F32)
                + jnp.dot(p_lo, ones, preferred_element_type=F32)).reshape(tc, nk, LANES)
        a = jnp.sum(sums * eye_ref[...], axis=1)
        hid_ref[c0:c0 + tc, :] = (0.5 * a * (1.0 + lax.erf(a * (2.0 ** -0.5)))) * gate_ref[c0:c0 + tc, :]


def _peer_up(x1r, g_ffn, rows, gate, tbl_u, *, tb):
    n = x1r.shape[0]
    nk = rows.shape[1]
    assert nk == LANES
    return pl.pallas_call(
        _peer_up_kernel,
        grid=(n // tb,),
        in_specs=[_smem_rows_spec(tb, nk),
                  pl.BlockSpec((tb, SUBLANES, LANES), lambda i: (i, 0, 0)),
                  _const_spec((SUBLANES, LANES)),
                  pl.BlockSpec((tb, nk), lambda i: (i, 0)),
                  _const_spec((LANES, LANES)), _const_spec((nk, LANES)),
                  _resident_spec(tbl_u.shape)],
        out_specs=pl.BlockSpec((tb, nk), lambda i: (i, 0)),
        out_shape=jax.ShapeDtypeStruct((n, nk), F32),
        scratch_shapes=[pltpu.VMEM((tb, SUBLANES, LANES), F32),
                        pltpu.VMEM((nk * ROW_WORDS, LANES), F32),
                        pltpu.VMEM((tb, nk, LANES), F32)],
        compiler_params=_cparams(1),
        name="peer_up",
    )(rows, x1r, g_ffn.reshape(SUBLANES, LANES), gate, jnp.ones((LANES, LANES), BF16),
      jnp.eye(nk, LANES, dtype=F32), tbl_u)


def _peer_down_kernel(rows_ref, hid_ref, x_ref, g_ref, tbl, y_ref, w_s, o_s):
    tb = x_ref.shape[0]
    nk = rows_ref.shape[1]
    n_acc = 4
    hid_t = hid_ref[...].T
    for c in range(tb):
        w_s[c] = jnp.broadcast_to(hid_t[:, c:c + 1], (nk, LANES))

    def token(c, carry):
        acc_lo = [jnp.zeros((ROW_WORDS, LANES), F32) for _ in range(n_acc)]
        acc_hi = [jnp.zeros((ROW_WORDS, LANES), F32) for _ in range(n_acc)]
        for r in range(nk):
            row = pl.multiple_of(rows_ref[c, r], ROW_WORDS)
            lo, hi = _unpack(tbl[pl.ds(row, ROW_WORDS), :])
            wgt = jnp.broadcast_to(w_s[c, r:r + 1, :], (ROW_WORDS, LANES))
            acc_lo[r % n_acc] = acc_lo[r % n_acc] + wgt * lo
            acc_hi[r % n_acc] = acc_hi[r % n_acc] + wgt * hi
        lo = (acc_lo[0] + acc_lo[1]) + (acc_lo[2] + acc_lo[3])
        hi = (acc_hi[0] + acc_hi[1]) + (acc_hi[2] + acc_hi[3])
        o_s[c] = x_ref[c] + jnp.concatenate([lo, hi], axis=0)
        return carry

    lax.fori_loop(0, tb, token, 0)
    xo = o_s[...]
    ss = jnp.sum(jnp.sum(xo * xo, axis=2, keepdims=True), axis=1, keepdims=True)
    y_ref[...] = (xo * lax.rsqrt(ss * (1.0 / (SUBLANES * LANES)) + EPS)) * g_ref[...]


def _peer_down(x1r, g_final, rows, hid, tbl_v, *, tb):
    n = x1r.shape[0]
    nk = rows.shape[1]
    tok = pl.BlockSpec((tb, SUBLANES, LANES), lambda i: (i, 0, 0))
    return pl.pallas_call(
        _peer_down_kernel,
        grid=(n // tb,),
        in_specs=[_smem_rows_spec(tb, nk), pl.BlockSpec((tb, nk), lambda i: (i, 0)), tok,
                  _const_spec((SUBLANES, LANES)), _resident_spec(tbl_v.shape)],
        out_specs=tok,
        out_shape=jax.ShapeDtypeStruct((n, SUBLANES, LANES), F32),
        scratch_shapes=[pltpu.VMEM((tb, nk, LANES), F32),
                        pltpu.VMEM((tb, SUBLANES, LANES), F32)],
        compiler_params=_cparams(1),
        name="peer_down",
    )(rows, hid, x1r, g_final.reshape(SUBLANES, LANES), tbl_v)


def _pick(n, pref):
    t = min(pref, n)
    while n % t:
        t //= 2
    return t


def _trunk(x, g_mix, w_in, pool_w, pool_scale, q_norm, k_norm, w_pool_out, w_attn_out, w_out,
           g_ffn, w_query, sub_keys, tbl_u, tbl_v, g_final):
    b, t, d = x.shape
    n = b * t
    x2d = x.reshape(n, d)
    tm = _pick(t, 512)
    p, q, k0, k1, v0, v1, gp, ga = _in_proj(x2d, t, g_mix, w_in, q_norm, k_norm, tm=tm)
    r3 = lambda a: a.reshape(b, t, a.shape[-1])
    attn = _attention(r3(q), r3(k0), r3(k1), r3(v0), r3(v1), tq=_pick(t, 128))
    x1 = _merge(x2d, t, p, attn.reshape(n, -1), gp, ga, pool_w, pool_scale, w_pool_out, w_attn_out,
                w_out, tm=tm)
    rows, gate = _peer_route(x1, g_ffn, w_query, sub_keys, tb=_pick(n, 256))
    x1r = x1.reshape(n, SUBLANES, LANES)
    tbp = _pick(n, 64)
    hid = _peer_up(x1r, g_ffn, rows, gate, tbl_u, tb=tbp)
    y = _peer_down(x1r, g_final, rows, hid, tbl_v, tb=tbp)
    return y.reshape(b, t, d)


def kernel(x_prompt, x_sample, g_mix, w_in, pool_w, pool_scale, q_norm, k_norm, w_pool_out, w_attn_out,
           w_out, g_ffn, w_query, sub_keys, expert_u, expert_v, g_final):
    assert g_mix.shape[0] == 1, "single-layer trunk"
    tbl_u = _pack_table(expert_u[0])
    tbl_v = _pack_table(expert_v[0])
    args = (g_mix[0], w_in[0], pool_w[0], pool_scale[0], q_norm[0], k_norm[0], w_pool_out[0],
            w_attn_out[0], w_out[0], g_ffn[0], w_query[0], sub_keys[0], tbl_u, tbl_v, g_final)
    return (_trunk(x_prompt, *args), _trunk(x_sample, *args))
```

```python
import functools
import math

import jax
import jax.numpy as jnp
from jax import lax
from jax.experimental import pallas as pl
from jax.experimental.pallas import tpu as pltpu

GRID_W = 64
POOL_WINDOWS = (2, 4, 8, 16)
POOL_GROUP = 128
N_HEADS = 8
N_KV_HEADS = 2
HEAD_DIM = 64
ROPE_THETA = 10000.0
N_KEYS = 128
PEER_HEADS = 8
PEER_HALF = 128
PEER_TOPK = 16
EPS = 1e-6

LANES = 128
SUBLANES = 8
VMEM_LIMIT_BYTES = 56 * 1024 * 1024

POOL_HALO = 16

F32 = jnp.float32
BF16 = jnp.bfloat16
NEG_INF = float("-inf")


def _cparams(n_axes):
    return pltpu.CompilerParams(
        dimension_semantics=("arbitrary",) * n_axes,
        vmem_limit_bytes=VMEM_LIMIT_BYTES,
    )


def _const_spec(shape):
    nd = len(shape)
    return pl.BlockSpec(shape, lambda *_: (0,) * nd)


def _in_proj_kernel(x_ref, g_ref, w_ref, qg_ref, qgp_ref, kg_ref, kgp_ref, cos_ref, sin_ref,
                    p_ref, q_ref, k0_ref, k1_ref, v0_ref, v1_ref, gp_ref, ga_ref, *, widths):
    x = x_ref[...]
    r = lax.rsqrt(jnp.mean(x * x, axis=-1, keepdims=True) + EPS)
    h = ((x * r) * g_ref[...]).astype(BF16)

    offs = {}
    o = 0
    for name, w in widths:
        offs[name] = (o, w)
        o += w

    def proj(name):
        a, w = offs[name]
        return jnp.dot(h, w_ref[:, a:a + w], preferred_element_type=F32)

    p_ref[...] = proj("p")

    cos = cos_ref[...]
    sin = sin_ref[...]
    lane = lax.broadcasted_iota(jnp.int32, (1, LANES), 1)
    first_head = lane < HEAD_DIM

    def norm_rope(z, zp, gain, gain_p):
        sq = z * z
        ss_a = jnp.sum(jnp.where(first_head, sq, 0.0), axis=-1, keepdims=True)
        ss_b = jnp.sum(jnp.where(first_head, 0.0, sq), axis=-1, keepdims=True)
        rr = lax.rsqrt(jnp.where(first_head, ss_a, ss_b) * (1.0 / HEAD_DIM) + EPS)
        return ((z * rr) * gain) * cos + ((zp * rr) * gain_p) * sin

    zq = proj("q")
    zqp = proj("qp")
    scale = HEAD_DIM ** -0.5 * math.log2(math.e)
    for c in range(zq.shape[1] // LANES):
        sl = slice(c * LANES, (c + 1) * LANES)
        qr = norm_rope(zq[:, sl], zqp[:, sl], qg_ref[...], qgp_ref[...])
        q_ref[:, sl] = (qr * scale).astype(BF16)

    kr = norm_rope(proj("k"), proj("kp"), kg_ref[...], kgp_ref[...]).astype(BF16)
    k0_ref[...] = kr[:, :HEAD_DIM]
    k1_ref[...] = kr[:, HEAD_DIM:]
    zv = proj("v").astype(BF16)
    v0_ref[...] = zv[:, :HEAD_DIM]
    v1_ref[...] = zv[:, HEAD_DIM:]
    gp_ref[...] = jax.nn.sigmoid(proj("gp")).astype(BF16)
    ga_ref[...] = jax.nn.sigmoid(proj("ga")).astype(BF16)


def _rope_partner_perm(n_heads):
    q = HEAD_DIM // 4
    base = jnp.concatenate([jnp.arange(q, 2 * q), jnp.arange(0, q),
                            jnp.arange(3 * q, 4 * q), jnp.arange(2 * q, 3 * q)])
    return (jnp.arange(n_heads)[:, None] * HEAD_DIM + base[None, :]).reshape(-1)


def _rope_tables(seq):
    quarter = HEAD_DIM // 4
    inv_freq = ROPE_THETA ** (-jnp.arange(quarter, dtype=F32) / quarter)
    t = jnp.arange(seq)
    ang_r = (t // GRID_W).astype(F32)[:, None] * inv_freq
    ang_c = (t % GRID_W).astype(F32)[:, None] * inv_freq
    cos = jnp.concatenate([jnp.cos(ang_r)] * 2 + [jnp.cos(ang_c)] * 2, axis=-1)
    sin = jnp.concatenate([-jnp.sin(ang_r), jnp.sin(ang_r), -jnp.sin(ang_c), jnp.sin(ang_c)], axis=-1)
    return jnp.tile(cos, (1, 2)), jnp.tile(sin, (1, 2))


def _in_proj(x2d, seq, g_mix, w_in, q_norm, k_norm, *, tm):
    n, d = x2d.shape
    pool_w = POOL_GROUP * len(POOL_WINDOWS)
    q_w = N_HEADS * HEAD_DIM
    kv_w = N_KV_HEADS * HEAD_DIM
    o1 = pool_w
    o2 = o1 + q_w
    o3 = o2 + kv_w
    o4 = o3 + kv_w
    o5 = o4 + d
    wq = w_in[:, o1:o2]
    wk = w_in[:, o2:o3]
    w_cat = jnp.concatenate(
        [w_in[:, :o1], wq, wq[:, _rope_partner_perm(N_HEADS)], wk, wk[:, _rope_partner_perm(N_KV_HEADS)],
         w_in[:, o3:o4], w_in[:, o4:o5], w_in[:, o5:]], axis=1).astype(BF16)
    widths = (("p", pool_w), ("q", q_w), ("qp", q_w), ("k", kv_w), ("kp", kv_w), ("v", kv_w),
              ("gp", d), ("ga", d))
    perm1 = _rope_partner_perm(1)
    qg = jnp.tile(q_norm, 2)[None, :]
    qgp = jnp.tile(q_norm[perm1], 2)[None, :]
    kg = jnp.tile(k_norm, 2)[None, :]
    kgp = jnp.tile(k_norm[perm1], 2)[None, :]
    cos, sin = _rope_tables(seq)
    nb_seq = seq // tm

    row = lambda w: pl.BlockSpec((tm, w), lambda i: (i, 0))
    tab = pl.BlockSpec((tm, LANES), lambda i: (i % nb_seq, 0))
    out_shape = (
        jax.ShapeDtypeStruct((n, pool_w), F32),
        jax.ShapeDtypeStruct((n, q_w), BF16),
        jax.ShapeDtypeStruct((n, HEAD_DIM), BF16), jax.ShapeDtypeStruct((n, HEAD_DIM), BF16),
        jax.ShapeDtypeStruct((n, HEAD_DIM), BF16), jax.ShapeDtypeStruct((n, HEAD_DIM), BF16),
        jax.ShapeDtypeStruct((n, d), BF16), jax.ShapeDtypeStruct((n, d), BF16),
    )
    return pl.pallas_call(
        functools.partial(_in_proj_kernel, widths=widths),
        grid=(n // tm,),
        in_specs=[row(d), _const_spec((1, d)), _const_spec(w_cat.shape),
                  _const_spec((1, LANES)), _const_spec((1, LANES)), _const_spec((1, LANES)),
                  _const_spec((1, LANES)), tab, tab],
        out_specs=(row(pool_w), row(q_w), row(HEAD_DIM), row(HEAD_DIM), row(HEAD_DIM), row(HEAD_DIM),
                   row(d), row(d)),
        out_shape=out_shape,
        compiler_params=_cparams(1),
        name="in_proj",
    )(x2d, g_mix[None, :], w_cat, qg, qgp, kg, kgp, cos, sin)


def _attn_kernel(q_ref, k0_ref, k1_ref, v0_ref, v1_ref, o_ref, *, ck):
    grp = N_HEADS // N_KV_HEADS
    q = q_ref[0]
    tq = q.shape[0]
    n_chunks = k0_ref.shape[1] // ck
    for g, (k_ref, v_ref) in enumerate(((k0_ref, v0_ref), (k1_ref, v1_ref))):
        qs = jnp.concatenate(
            [q[:, (g * grp + j) * HEAD_DIM:(g * grp + j + 1) * HEAD_DIM] for j in range(grp)], axis=0)

        def scores(j):
            return lax.dot_general(qs, k_ref[0, j * ck:(j + 1) * ck, :], (((1,), (1,)), ((), ())),
                                   preferred_element_type=F32)

        m = jnp.max(scores(0), axis=-1, keepdims=True)
        for j in range(1, n_chunks):
            m = jnp.maximum(m, jnp.max(scores(j), axis=-1, keepdims=True))
        l = jnp.zeros_like(m)
        acc = jnp.zeros((qs.shape[0], HEAD_DIM), F32)
        for j in range(n_chunks):
            e = jnp.exp2(scores(j) - m)
            l = l + jnp.sum(e, axis=-1, keepdims=True)
            acc = acc + jnp.dot(e.astype(BF16), v_ref[0, j * ck:(j + 1) * ck, :], preferred_element_type=F32)
        o = (acc / l).astype(BF16)
        for j in range(grp):
            h = g * grp + j
            o_ref[0, :, h * HEAD_DIM:(h + 1) * HEAD_DIM] = o[j * tq:(j + 1) * tq]


def _attention(q, k0, k1, v0, v1, *, tq):
    b, t, qw = q.shape
    kv = pl.BlockSpec((1, t, HEAD_DIM), lambda bi, i: (bi, 0, 0))
    qs = pl.BlockSpec((1, tq, qw), lambda bi, i: (bi, i, 0))
    return pl.pallas_call(
        functools.partial(_attn_kernel, ck=_pick(t, 1024)),
        grid=(b, t // tq),
        in_specs=[qs, kv, kv, kv, kv],
        out_specs=qs,
        out_shape=jax.ShapeDtypeStruct((b, t, qw), BF16),
        compiler_params=_cparams(2),
        name="attention",
    )(q, k0, k1, v0, v1)


def _merge_kernel(x_ref, p_ref, prev_ref, next_ref, attn_ref, gp_ref, ga_ref, pw_ref, ps_ref,
                  wpo_ref, wao_ref, wo_ref, o_ref, e_ref, *, nb_seq, seq):
    i = pl.program_id(0)
    tm = x_ref.shape[0]
    blk = i % nb_seq
    halo = POOL_HALO
    e_ref[0:halo, :] = jnp.where(blk == 0, 0.0, prev_ref[...])
    e_ref[halo:halo + tm, :] = p_ref[...]
    e_ref[halo + tm:, :] = jnp.where(blk == nb_seq - 1, 0.0, next_ref[...])

    t = blk * tm + lax.broadcasted_iota(jnp.int32, (tm, 1), 0)
    mixed = []
    for gi, w in enumerate(POOL_WINDOWS):
        cols = slice(gi * POOL_GROUP, (gi + 1) * POOL_GROUP)
        acc = None
        for dlt in range(-(w // 2), w - w // 2):
            piece = e_ref[halo + dlt:halo + dlt + tm, cols]
            acc = piece if acc is None else acc + piece
        cnt = (jnp.minimum(t + (w - w // 2), seq) - jnp.maximum(t - w // 2, 0)).astype(F32)
        pooled = (acc / cnt - e_ref[halo:halo + tm, cols]).astype(BF16)
        mixed.append(jnp.dot(pooled, pw_ref[gi], preferred_element_type=F32) * ps_ref[:, cols])
    mixed = jnp.concatenate(mixed, axis=-1).astype(BF16)
    pool_branch = jnp.dot(mixed, wpo_ref[...], preferred_element_type=F32)
    attn_branch = jnp.dot(attn_ref[...], wao_ref[...], preferred_element_type=F32)
    merged = gp_ref[...].astype(F32) * pool_branch + ga_ref[...].astype(F32) * attn_branch
    o_ref[...] = x_ref[...] + jnp.dot(merged.astype(BF16), wo_ref[...], preferred_element_type=F32)


def _merge(x2d, seq, p, attn, gp, ga, pool_w, pool_scale, w_pool_out, w_attn_out, w_out, *, tm):
    n, d = x2d.shape
    pw = p.shape[1]
    nb_seq = seq // tm
    hb = tm // POOL_HALO
    n_halo_blocks = n // POOL_HALO
    row = lambda w: pl.BlockSpec((tm, w), lambda i: (i, 0))
    prev = pl.BlockSpec((POOL_HALO, pw), lambda i: (jnp.maximum(i * hb - 1, 0), 0))
    nxt = pl.BlockSpec((POOL_HALO, pw), lambda i: (jnp.minimum((i + 1) * hb, n_halo_blocks - 1), 0))
    return pl.pallas_call(
        functools.partial(_merge_kernel, nb_seq=nb_seq, seq=seq),
        grid=(n // tm,),
        in_specs=[row(d), row(pw), prev, nxt, row(attn.shape[1]), row(d), row(d),
                  _const_spec(pool_w.shape), _const_spec((1, pw)), _const_spec(w_pool_out.shape),
                  _const_spec(w_attn_out.shape), _const_spec(w_out.shape)],
        out_specs=row(d),
        out_shape=jax.ShapeDtypeStruct((n, d), F32),
        scratch_shapes=[pltpu.VMEM((tm + 2 * POOL_HALO, pw), F32)],
        compiler_params=_cparams(1),
        name="merge",
    )(x2d, p, p, p, attn, gp, ga, pool_w.astype(BF16), pool_scale[None, :], w_pool_out.astype(BF16),
      w_attn_out.astype(BF16), w_out.astype(BF16))


def _topk_rows(vals, k, payload=None):
    n = vals.shape[0]
    iota = lax.broadcasted_iota(jnp.int32, vals.shape, 0).astype(F32)
    cur = vals
    out_v, out_p = [], []
    for _ in range(k):
        m = jnp.max(cur, axis=0, keepdims=True)
        pos = jnp.min(jnp.where(cur == m, iota, float(n)), axis=0, keepdims=True)
        hit = iota == pos
        out_v.append(m)
        if payload is None:
            out_p.append(pos)
        else:
            out_p.append(jnp.sum(jnp.where(hit, payload, 0.0), axis=0, keepdims=True))
        cur = jnp.where(hit, NEG_INF, cur)
    return jnp.concatenate(out_v, axis=0), jnp.concatenate(out_p, axis=0)


def _candidate_pairs():
    return [(a, b) for a in range(PEER_TOPK) for b in range(PEER_TOPK) if (a + 1) * (b + 1) <= PEER_TOPK]


def _pair_selectors():
    pairs = _candidate_pairs()
    n = len(pairs) + (-len(pairs) % SUBLANES)
    rows = jnp.arange(len(pairs))
    sel_a = jnp.zeros((n, PEER_TOPK), F32).at[rows, jnp.array([p[0] for p in pairs])].set(1.0)
    sel_b = jnp.zeros((n, PEER_TOPK), F32).at[rows, jnp.array([p[1] for p in pairs])].set(1.0)
    pad = jnp.where(jnp.arange(n) < len(pairs), 0.0, NEG_INF).astype(F32)[:, None]
    return sel_a, sel_b, pad


def _select_rows(sel, x):
    return jnp.dot(sel, x, precision=lax.Precision.HIGHEST, preferred_element_type=F32)


def _peer_route_kernel(x_ref, g_ref, wq_ref, keys_ref, sela_ref, selb_ref, pad_ref, rows_ref, gate_ref):
    x = x_ref[...]
    r = lax.rsqrt(jnp.mean(x * x, axis=-1, keepdims=True) + EPS)
    h = ((x * r) * g_ref[...]).astype(BF16)
    sel_a = sela_ref[...]
    sel_b = selb_ref[...]
    row_list, gate_rows = [], []
    for hd in range(PEER_HEADS):
        sv, si = [], []
        for part in range(2):
            c0 = (hd * 2 + part) * PEER_HALF
            qh = jnp.dot(h, wq_ref[:, c0:c0 + PEER_HALF], preferred_element_type=F32).astype(BF16)
            s_t = lax.dot_general(keys_ref[hd * 2 + part], qh, (((1,), (1,)), ((), ())),
                                  preferred_element_type=F32)
            v, ix = _topk_rows(s_t, PEER_TOPK)
            sv.append(v)
            si.append(ix)
        cand = (_select_rows(sel_a, sv[0]) + _select_rows(sel_b, sv[1])) + pad_ref[...]
        cidx = _select_rows(sel_a, si[0]) * float(N_KEYS) + _select_rows(sel_b, si[1])
        top_s, idx = _topk_rows(cand, PEER_TOPK, payload=cidx)
        e = jnp.exp(top_s - top_s[0:1])
        gate_rows.append(e / jnp.sum(e, axis=0, keepdims=True))
        row_list.append(idx)
    rows_ref[...] = (jnp.concatenate(row_list, axis=0).astype(jnp.int32) * ROW_WORDS).T
    gate_ref[...] = jnp.concatenate(gate_rows, axis=0).T


def _peer_route(x1, g_ffn, w_query, sub_keys, *, tb):
    n, d = x1.shape
    nk = PEER_HEADS * PEER_TOPK
    keys = sub_keys.reshape(PEER_HEADS * 2, N_KEYS, PEER_HALF).astype(BF16)
    sel_a, sel_b, pad = _pair_selectors()
    row = lambda w: pl.BlockSpec((tb, w), lambda i: (i, 0))
    return pl.pallas_call(
        _peer_route_kernel,
        grid=(n // tb,),
        in_specs=[row(d), _const_spec((1, d)), _const_spec(w_query.shape), _const_spec(keys.shape),
                  _const_spec(sel_a.shape), _const_spec(sel_b.shape), _const_spec(pad.shape)],
        out_specs=(row(nk), row(nk)),
        out_shape=(jax.ShapeDtypeStruct((n, nk), jnp.int32), jax.ShapeDtypeStruct((n, nk), F32)),
        compiler_params=_cparams(1),
        name="peer_route",
    )(x1, g_ffn[None, :], w_query.astype(BF16), keys, sel_a, sel_b, pad)


ROW_WORDS = 4


def _pack_table(tbl):
    e, d = tbl.shape
    bits = lax.bitcast_convert_type(tbl.astype(BF16), jnp.uint16).astype(jnp.uint32)
    half = d // 2
    words = bits[:, :half] | (bits[:, half:] << 16)
    return lax.bitcast_convert_type(words, jnp.int32).reshape(e * ROW_WORDS, LANES)


def _unpack(w):
    lo = lax.bitcast_convert_type(w << 16, F32)
    hi = lax.bitcast_convert_type(w & jnp.int32(-65536), F32)
    return lo, hi


def _resident_spec(shape):
    nd = len(shape)
    return pl.BlockSpec(shape, lambda *_: (0,) * nd, pipeline_mode=pl.Buffered(1))


def _smem_rows_spec(tb, nk):
    return pl.BlockSpec((tb, nk), lambda i: (i, 0), memory_space=pltpu.SMEM)


def _peer_up_kernel(rows_ref, x_ref, g_ref, gate_ref, ones_ref, eye_ref, tbl, hid_ref, h_s, s_s, a_s):
    x = x_ref[...]
    ss = jnp.sum(jnp.sum(x * x, axis=2, keepdims=True), axis=1, keepdims=True)
    h_s[...] = (x * lax.rsqrt(ss * (1.0 / (SUBLANES * LANES)) + EPS)) * g_ref[...]
    tb = x.shape[0]
    nk = rows_ref.shape[1]

    def token(c, carry):
        xv = h_s[c]
        xl = xv[:ROW_WORDS]
        xh = xv[ROW_WORDS:]
        for r in range(nk):
            row = pl.multiple_of(rows_ref[c, r], ROW_WORDS)
            lo, hi = _unpack(tbl[pl.ds(row, ROW_WORDS), :])
            s_s[r * ROW_WORDS:(r + 1) * ROW_WORDS, :] = lo * xl + hi * xh
        part = s_s[pl.ds(0, nk, stride=ROW_WORDS), :]
        for j in range(1, ROW_WORDS):
            part = part + s_s[pl.ds(j, nk, stride=ROW_WORDS), :]
        a_s[c] = part
        return carry

    lax.fori_loop(0, tb, token, 0)
    ones = ones_ref[...]
    tc = min(tb, 16)
    for c0 in range(0, tb, tc):
        part = a_s[c0:c0 + tc].reshape(tc * nk, LANES)
        p_hi = part.astype(BF16)
        p_lo = (part - p_hi.astype(F32)).astype(BF16)
        sums = (jnp.dot(p_hi, ones, preferred_element_type=F32)
                + jnp.dot(p_lo, ones, preferred_element_type=F32)).reshape(tc, nk, LANES)
        a = jnp.sum(sums * eye_ref[...], axis=1)
        hid_ref[c0:c0 + tc, :] = (0.5 * a * (1.0 + lax.erf(a * (2.0 ** -0.5)))) * gate_ref[c0:c0 + tc, :]


def _peer_up(x1r, g_ffn, rows, gate, tbl_u, *, tb):
    n = x1r.shape[0]
    nk = rows.shape[1]
    assert nk == LANES
    return pl.pallas_call(
        _peer_up_kernel,
        grid=(n // tb,),
        in_specs=[_smem_rows_spec(tb, nk),
                  pl.BlockSpec((tb, SUBLANES, LANES), lambda i: (i, 0, 0)),
                  _const_spec((SUBLANES, LANES)),
                  pl.BlockSpec((tb, nk), lambda i: (i, 0)),
                  _const_spec((LANES, LANES)), _const_spec((nk, LANES)),
                  _resident_spec(tbl_u.shape)],
        out_specs=pl.BlockSpec((tb, nk), lambda i: (i, 0)),
        out_shape=jax.ShapeDtypeStruct((n, nk), F32),
        scratch_shapes=[pltpu.VMEM((tb, SUBLANES, LANES), F32),
                        pltpu.VMEM((nk * ROW_WORDS, LANES), F32),
                        pltpu.VMEM((tb, nk, LANES), F32)],
        compiler_params=_cparams(1),
        name="peer_up",
    )(rows, x1r, g_ffn.reshape(SUBLANES, LANES), gate, jnp.ones((LANES, LANES), BF16),
      jnp.eye(nk, LANES, dtype=F32), tbl_u)


def _peer_down_kernel(rows_ref, hid_ref, x_ref, g_ref, ones_ref, eye_ref, tbl, y_ref, w_s, o_s):
    tb = x_ref.shape[0]
    nk = rows_ref.shape[1]
    n_acc = 4
    eye = eye_ref[...]
    ones = ones_ref[...]
    unroll = 4

    def spread(i, carry):
        for u in range(unroll):
            c = i * unroll + u
            d = eye * hid_ref[pl.ds(c, 1), :]
            d_hi = d.astype(BF16)
            d_lo = (d - d_hi.astype(F32)).astype(BF16)
            w_s[c] = (jnp.dot(d_hi, ones, preferred_element_type=F32)
                      + jnp.dot(d_lo, ones, preferred_element_type=F32))
        return carry

    lax.fori_loop(0, tb // unroll, spread, 0)

    def token(c, carry):
        acc_lo = [jnp.zeros((ROW_WORDS, LANES), F32) for _ in range(n_acc)]
        acc_hi = [jnp.zeros((ROW_WORDS, LANES), F32) for _ in range(n_acc)]
        for r in range(nk):
            row = pl.multiple_of(rows_ref[c, r], ROW_WORDS)
            lo, hi = _unpack(tbl[pl.ds(row, ROW_WORDS), :])
            wgt = jnp.broadcast_to(w_s[c, r:r + 1, :], (ROW_WORDS, LANES))
            acc_lo[r % n_acc] = acc_lo[r % n_acc] + wgt * lo
            acc_hi[r % n_acc] = acc_hi[r % n_acc] + wgt * hi
        lo = (acc_lo[0] + acc_lo[1]) + (acc_lo[2] + acc_lo[3])
        hi = (acc_hi[0] + acc_hi[1]) + (acc_hi[2] + acc_hi[3])
        o_s[c] = x_ref[c] + jnp.concatenate([lo, hi], axis=0)
        return carry

    lax.fori_loop(0, tb, token, 0)
    xo = o_s[...]
    ss = jnp.sum(jnp.sum(xo * xo, axis=2, keepdims=True), axis=1, keepdims=True)
    y_ref[...] = (xo * lax.rsqrt(ss * (1.0 / (SUBLANES * LANES)) + EPS)) * g_ref[...]


def _peer_down(x1r, g_final, rows, hid, tbl_v, *, tb):
    n = x1r.shape[0]
    nk = rows.shape[1]
    tok = pl.BlockSpec((tb, SUBLANES, LANES), lambda i: (i, 0, 0))
    return pl.pallas_call(
        _peer_down_kernel,
        grid=(n // tb,),
        in_specs=[_smem_rows_spec(tb, nk), pl.BlockSpec((tb, nk), lambda i: (i, 0)), tok,
                  _const_spec((SUBLANES, LANES)), _const_spec((LANES, LANES)), _const_spec((nk, LANES)),
                  _resident_spec(tbl_v.shape)],
        out_specs=tok,
        out_shape=jax.ShapeDtypeStruct((n, SUBLANES, LANES), F32),
        scratch_shapes=[pltpu.VMEM((tb, nk, LANES), F32),
                        pltpu.VMEM((tb, SUBLANES, LANES), F32)],
        compiler_params=_cparams(1),
        name="peer_down",
    )(rows, hid, x1r, g_final.reshape(SUBLANES, LANES), jnp.ones((LANES, LANES), BF16),
      jnp.eye(nk, LANES, dtype=F32), tbl_v)


def _pick(n, pref):
    t = min(pref, n)
    while n % t:
        t //= 2
    return t


def _trunk(x, g_mix, w_in, pool_w, pool_scale, q_norm, k_norm, w_pool_out, w_attn_out, w_out,
           g_ffn, w_query, sub_keys, tbl_u, tbl_v, g_final):
    b, t, d = x.shape
    n = b * t
    x2d = x.reshape(n, d)
    tm = _pick(t, 512)
    p, q, k0, k1, v0, v1, gp, ga = _in_proj(x2d, t, g_mix, w_in, q_norm, k_norm, tm=tm)
    r3 = lambda a: a.reshape(b, t, a.shape[-1])
    attn = _attention(r3(q), r3(k0), r3(k1), r3(v0), r3(v1), tq=_pick(t, 128))
    x1 = _merge(x2d, t, p, attn.reshape(n, -1), gp, ga, pool_w, pool_scale, w_pool_out, w_attn_out,
                w_out, tm=tm)
    rows, gate = _peer_route(x1, g_ffn, w_query, sub_keys, tb=_pick(n, LANES))
    x1r = x1.reshape(n, SUBLANES, LANES)
    tbp = _pick(n, 64)
    hid = _peer_up(x1r, g_ffn, rows, gate, tbl_u, tb=tbp)
    y = _peer_down(x1r, g_final, rows, hid, tbl_v, tb=tbp)
    return y.reshape(b, t, d)


def kernel(x_prompt, x_sample, g_mix, w_in, pool_w, pool_scale, q_norm, k_norm, w_pool_out, w_attn_out,
           w_out, g_ffn, w_query, sub_keys, expert_u, expert_v, g_final):
    assert g_mix.shape[0] == 1, "single-layer trunk"
    tbl_u = _pack_table(expert_u[0])
    tbl_v = _pack_table(expert_v[0])
    args = (g_mix[0], w_in[0], pool_w[0], pool_scale[0], q_norm[0], k_norm[0], w_pool_out[0],
            w_attn_out[0], w_out[0], g_ffn[0], w_query[0], sub_keys[0], tbl_u, tbl_v, g_final)
    return (_trunk(x_prompt, *args), _trunk(x_sample, *args))
```

```python
import functools
import math

import jax
import jax.numpy as jnp
from jax import lax
from jax.experimental import pallas as pl
from jax.experimental.pallas import tpu as pltpu

GRID_W = 64
POOL_WINDOWS = (2, 4, 8, 16)
POOL_GROUP = 128
N_HEADS = 8
N_KV_HEADS = 2
HEAD_DIM = 64
ROPE_THETA = 10000.0
N_KEYS = 128
PEER_HEADS = 8
PEER_HALF = 128
PEER_TOPK = 16
EPS = 1e-6

LANES = 128
SUBLANES = 8
VMEM_LIMIT_BYTES = 56 * 1024 * 1024

POOL_HALO = 16

F32 = jnp.float32
BF16 = jnp.bfloat16
NEG_INF = float("-inf")


def _cparams(n_axes):
    return pltpu.CompilerParams(
        dimension_semantics=("arbitrary",) * n_axes,
        vmem_limit_bytes=VMEM_LIMIT_BYTES,
    )


def _const_spec(shape):
    nd = len(shape)
    return pl.BlockSpec(shape, lambda *_: (0,) * nd)


def _in_proj_kernel(x_ref, g_ref, w_ref, qg_ref, qgp_ref, kg_ref, kgp_ref, cos_ref, sin_ref,
                    p_ref, q_ref, k0_ref, k1_ref, v0_ref, v1_ref, gp_ref, ga_ref, *, widths):
    x = x_ref[...]
    r = lax.rsqrt(jnp.mean(x * x, axis=-1, keepdims=True) + EPS)
    h = ((x * r) * g_ref[...]).astype(BF16)

    offs = {}
    o = 0
    for name, w in widths:
        offs[name] = (o, w)
        o += w

    def proj(name):
        a, w = offs[name]
        return jnp.dot(h, w_ref[:, a:a + w], preferred_element_type=F32)

    p_ref[...] = proj("p")

    cos = cos_ref[...]
    sin = sin_ref[...]
    lane = lax.broadcasted_iota(jnp.int32, (1, LANES), 1)
    first_head = lane < HEAD_DIM

    def norm_rope(z, zp, gain, gain_p):
        sq = z * z
        ss_a = jnp.sum(jnp.where(first_head, sq, 0.0), axis=-1, keepdims=True)
        ss_b = jnp.sum(jnp.where(first_head, 0.0, sq), axis=-1, keepdims=True)
        rr = lax.rsqrt(jnp.where(first_head, ss_a, ss_b) * (1.0 / HEAD_DIM) + EPS)
        return ((z * rr) * gain) * cos + ((zp * rr) * gain_p) * sin

    zq = proj("q")
    zqp = proj("qp")
    scale = HEAD_DIM ** -0.5 * math.log2(math.e)
    for c in range(zq.shape[1] // LANES):
        sl = slice(c * LANES, (c + 1) * LANES)
        qr = norm_rope(zq[:, sl], zqp[:, sl], qg_ref[...], qgp_ref[...])
        q_ref[:, sl] = (qr * scale).astype(BF16)

    kr = norm_rope(proj("k"), proj("kp"), kg_ref[...], kgp_ref[...]).astype(BF16)
    k0_ref[...] = kr[:, :HEAD_DIM]
    k1_ref[...] = kr[:, HEAD_DIM:]
    zv = proj("v").astype(BF16)
    v0_ref[...] = zv[:, :HEAD_DIM]
    v1_ref[...] = zv[:, HEAD_DIM:]
    gp_ref[...] = jax.nn.sigmoid(proj("gp")).astype(BF16)
    ga_ref[...] = jax.nn.sigmoid(proj("ga")).astype(BF16)


def _rope_partner_perm(n_heads):
    q = HEAD_DIM // 4
    base = jnp.concatenate([jnp.arange(q, 2 * q), jnp.arange(0, q),
                            jnp.arange(3 * q, 4 * q), jnp.arange(2 * q, 3 * q)])
    return (jnp.arange(n_heads)[:, None] * HEAD_DIM + base[None, :]).reshape(-1)


def _rope_tables(seq):
    quarter = HEAD_DIM // 4
    inv_freq = ROPE_THETA ** (-jnp.arange(quarter, dtype=F32) / quarter)
    t = jnp.arange(seq)
    ang_r = (t // GRID_W).astype(F32)[:, None] * inv_freq
    ang_c = (t % GRID_W).astype(F32)[:, None] * inv_freq
    cos = jnp.concatenate([jnp.cos(ang_r)] * 2 + [jnp.cos(ang_c)] * 2, axis=-1)
    sin = jnp.concatenate([-jnp.sin(ang_r), jnp.sin(ang_r), -jnp.sin(ang_c), jnp.sin(ang_c)], axis=-1)
    return jnp.tile(cos, (1, 2)), jnp.tile(sin, (1, 2))


def _in_proj(x2d, seq, g_mix, w_in, q_norm, k_norm, *, tm):
    n, d = x2d.shape
    pool_w = POOL_GROUP * len(POOL_WINDOWS)
    q_w = N_HEADS * HEAD_DIM
    kv_w = N_KV_HEADS * HEAD_DIM
    o1 = pool_w
    o2 = o1 + q_w
    o3 = o2 + kv_w
    o4 = o3 + kv_w
    o5 = o4 + d
    wq = w_in[:, o1:o2]
    wk = w_in[:, o2:o3]
    w_cat = jnp.concatenate(
        [w_in[:, :o1], wq, wq[:, _rope_partner_perm(N_HEADS)], wk, wk[:, _rope_partner_perm(N_KV_HEADS)],
         w_in[:, o3:o4], w_in[:, o4:o5], w_in[:, o5:]], axis=1).astype(BF16)
    widths = (("p", pool_w), ("q", q_w), ("qp", q_w), ("k", kv_w), ("kp", kv_w), ("v", kv_w),
              ("gp", d), ("ga", d))
    perm1 = _rope_partner_perm(1)
    qg = jnp.tile(q_norm, 2)[None, :]
    qgp = jnp.tile(q_norm[perm1], 2)[None, :]
    kg = jnp.tile(k_norm, 2)[None, :]
    kgp = jnp.tile(k_norm[perm1], 2)[None, :]
    cos, sin = _rope_tables(seq)
    nb_seq = seq // tm

    row = lambda w: pl.BlockSpec((tm, w), lambda i: (i, 0))
    tab = pl.BlockSpec((tm, LANES), lambda i: (i % nb_seq, 0))
    out_shape = (
        jax.ShapeDtypeStruct((n, pool_w), F32),
        jax.ShapeDtypeStruct((n, q_w), BF16),
        jax.ShapeDtypeStruct((n, HEAD_DIM), BF16), jax.ShapeDtypeStruct((n, HEAD_DIM), BF16),
        jax.ShapeDtypeStruct((n, HEAD_DIM), BF16), jax.ShapeDtypeStruct((n, HEAD_DIM), BF16),
        jax.ShapeDtypeStruct((n, d), BF16), jax.ShapeDtypeStruct((n, d), BF16),
    )
    return pl.pallas_call(
        functools.partial(_in_proj_kernel, widths=widths),
        grid=(n // tm,),
        in_specs=[row(d), _const_spec((1, d)), _const_spec(w_cat.shape),
                  _const_spec((1, LANES)), _const_spec((1, LANES)), _const_spec((1, LANES)),
                  _const_spec((1, LANES)), tab, tab],
        out_specs=(row(pool_w), row(q_w), row(HEAD_DIM), row(HEAD_DIM), row(HEAD_DIM), row(HEAD_DIM),
                   row(d), row(d)),
        out_shape=out_shape,
        compiler_params=_cparams(1),
        name="in_proj",
    )(x2d, g_mix[None, :], w_cat, qg, qgp, kg, kgp, cos, sin)


def _attn_kernel(q_ref, k0_ref, k1_ref, v0_ref, v1_ref, o_ref, *, ck):
    grp = N_HEADS // N_KV_HEADS
    q = q_ref[0]
    tq = q.shape[0]
    n_chunks = k0_ref.shape[1] // ck
    for g, (k_ref, v_ref) in enumerate(((k0_ref, v0_ref), (k1_ref, v1_ref))):
        qs = jnp.concatenate(
            [q[:, (g * grp + j) * HEAD_DIM:(g * grp + j + 1) * HEAD_DIM] for j in range(grp)], axis=0)

        def scores(j):
            return lax.dot_general(qs, k_ref[0, j * ck:(j + 1) * ck, :], (((1,), (1,)), ((), ())),
                                   preferred_element_type=F32)

        m = jnp.max(scores(0), axis=-1, keepdims=True)
        for j in range(1, n_chunks):
            m = jnp.maximum(m, jnp.max(scores(j), axis=-1, keepdims=True))
        l = jnp.zeros_like(m)
        acc = jnp.zeros((qs.shape[0], HEAD_DIM), F32)
        for j in range(n_chunks):
            e = jnp.exp2(scores(j) - m)
            l = l + jnp.sum(e, axis=-1, keepdims=True)
            acc = acc + jnp.dot(e.astype(BF16), v_ref[0, j * ck:(j + 1) * ck, :], preferred_element_type=F32)
        o = (acc / l).astype(BF16)
        for j in range(grp):
            h = g * grp + j
            o_ref[0, :, h * HEAD_DIM:(h + 1) * HEAD_DIM] = o[j * tq:(j + 1) * tq]


def _attention(q, k0, k1, v0, v1, *, tq):
    b, t, qw = q.shape
    kv = pl.BlockSpec((1, t, HEAD_DIM), lambda bi, i: (bi, 0, 0))
    qs = pl.BlockSpec((1, tq, qw), lambda bi, i: (bi, i, 0))
    return pl.pallas_call(
        functools.partial(_attn_kernel, ck=_pick(t, 512)),
        grid=(b, t // tq),
        in_specs=[qs, kv, kv, kv, kv],
        out_specs=qs,
        out_shape=jax.ShapeDtypeStruct((b, t, qw), BF16),
        compiler_params=_cparams(2),
        name="attention",
    )(q, k0, k1, v0, v1)


def _merge_kernel(x_ref, p_ref, prev_ref, next_ref, attn_ref, gp_ref, ga_ref, pw_ref, ps_ref,
                  wpo_ref, wao_ref, wo_ref, o_ref, e_ref, *, nb_seq, seq):
    i = pl.program_id(0)
    tm = x_ref.shape[0]
    blk = i % nb_seq
    halo = POOL_HALO
    e_ref[0:halo, :] = jnp.where(blk == 0, 0.0, prev_ref[...])
    e_ref[halo:halo + tm, :] = p_ref[...]
    e_ref[halo + tm:, :] = jnp.where(blk == nb_seq - 1, 0.0, next_ref[...])

    t = blk * tm + lax.broadcasted_iota(jnp.int32, (tm, 1), 0)
    mixed = []
    for gi, w in enumerate(POOL_WINDOWS):
        cols = slice(gi * POOL_GROUP, (gi + 1) * POOL_GROUP)
        acc = None
        for dlt in range(-(w // 2), w - w // 2):
            piece = e_ref[halo + dlt:halo + dlt + tm, cols]
            acc = piece if acc is None else acc + piece
        cnt = (jnp.minimum(t + (w - w // 2), seq) - jnp.maximum(t - w // 2, 0)).astype(F32)
        pooled = (acc / cnt - e_ref[halo:halo + tm, cols]).astype(BF16)
        mixed.append(jnp.dot(pooled, pw_ref[gi], preferred_element_type=F32) * ps_ref[:, cols])
    mixed = jnp.concatenate(mixed, axis=-1).astype(BF16)
    pool_branch = jnp.dot(mixed, wpo_ref[...], preferred_element_type=F32)
    attn_branch = jnp.dot(attn_ref[...], wao_ref[...], preferred_element_type=F32)
    merged = gp_ref[...].astype(F32) * pool_branch + ga_ref[...].astype(F32) * attn_branch
    o_ref[...] = x_ref[...] + jnp.dot(merged.astype(BF16), wo_ref[...], preferred_element_type=F32)


def _merge(x2d, seq, p, attn, gp, ga, pool_w, pool_scale, w_pool_out, w_attn_out, w_out, *, tm):
    n, d = x2d.shape
    pw = p.shape[1]
    nb_seq = seq // tm
    hb = tm // POOL_HALO
    n_halo_blocks = n // POOL_HALO
    row = lambda w: pl.BlockSpec((tm, w), lambda i: (i, 0))
    prev = pl.BlockSpec((POOL_HALO, pw), lambda i: (jnp.maximum(i * hb - 1, 0), 0))
    nxt = pl.BlockSpec((POOL_HALO, pw), lambda i: (jnp.minimum((i + 1) * hb, n_halo_blocks - 1), 0))
    return pl.pallas_call(
        functools.partial(_merge_kernel, nb_seq=nb_seq, seq=seq),
        grid=(n // tm,),
        in_specs=[row(d), row(pw), prev, nxt, row(attn.shape[1]), row(d), row(d),
                  _const_spec(pool_w.shape), _const_spec((1, pw)), _const_spec(w_pool_out.shape),
                  _const_spec(w_attn_out.shape), _const_spec(w_out.shape)],
        out_specs=row(d),
        out_shape=jax.ShapeDtypeStruct((n, d), F32),
        scratch_shapes=[pltpu.VMEM((tm + 2 * POOL_HALO, pw), F32)],
        compiler_params=_cparams(1),
        name="merge",
    )(x2d, p, p, p, attn, gp, ga, pool_w.astype(BF16), pool_scale[None, :], w_pool_out.astype(BF16),
      w_attn_out.astype(BF16), w_out.astype(BF16))


def _topk_rows(vals, k, payload=None):
    n = vals.shape[0]
    iota = lax.broadcasted_iota(jnp.int32, vals.shape, 0).astype(F32)
    cur = vals
    out_v, out_p = [], []
    for _ in range(k):
        m = jnp.max(cur, axis=0, keepdims=True)
        pos = jnp.min(jnp.where(cur == m, iota, float(n)), axis=0, keepdims=True)
        hit = iota == pos
        out_v.append(m)
        if payload is None:
            out_p.append(pos)
        else:
            out_p.append(jnp.sum(jnp.where(hit, payload, 0.0), axis=0, keepdims=True))
        cur = jnp.where(hit, NEG_INF, cur)
    return jnp.concatenate(out_v, axis=0), jnp.concatenate(out_p, axis=0)


def _candidate_pairs():
    return [(a, b) for a in range(PEER_TOPK) for b in range(PEER_TOPK) if (a + 1) * (b + 1) <= PEER_TOPK]


def _pair_selectors():
    pairs = _candidate_pairs()
    n = len(pairs) + (-len(pairs) % SUBLANES)
    rows = jnp.arange(len(pairs))
    sel_a = jnp.zeros((n, PEER_TOPK), F32).at[rows, jnp.array([p[0] for p in pairs])].set(1.0)
    sel_b = jnp.zeros((n, PEER_TOPK), F32).at[rows, jnp.array([p[1] for p in pairs])].set(1.0)
    pad = jnp.where(jnp.arange(n) < len(pairs), 0.0, NEG_INF).astype(F32)[:, None]
    return sel_a, sel_b, pad


def _select_rows(sel, x):
    return jnp.dot(sel, x, precision=lax.Precision.HIGHEST, preferred_element_type=F32)


def _peer_route_kernel(x_ref, g_ref, wq_ref, keys_ref, sela_ref, selb_ref, pad_ref, rows_ref, gate_ref):
    x = x_ref[...]
    r = lax.rsqrt(jnp.mean(x * x, axis=-1, keepdims=True) + EPS)
    h = ((x * r) * g_ref[...]).astype(BF16)
    sel_a = sela_ref[...]
    sel_b = selb_ref[...]
    n_tiles = x.shape[0] // LANES
    row_list = [[] for _ in range(n_tiles)]
    gate_rows = [[] for _ in range(n_tiles)]
    for hd in range(PEER_HEADS):
        sv = [[] for _ in range(n_tiles)]
        si = [[] for _ in range(n_tiles)]
        for part in range(2):
            c0 = (hd * 2 + part) * PEER_HALF
            qh = jnp.dot(h, wq_ref[:, c0:c0 + PEER_HALF], preferred_element_type=F32).astype(BF16)
            s_t = lax.dot_general(keys_ref[hd * 2 + part], qh, (((1,), (1,)), ((), ())),
                                  preferred_element_type=F32)
            for lt in range(n_tiles):
                v, ix = _topk_rows(s_t[:, lt * LANES:(lt + 1) * LANES], PEER_TOPK)
                sv[lt].append(v)
                si[lt].append(ix)
        for lt in range(n_tiles):
            cand = (_select_rows(sel_a, sv[lt][0]) + _select_rows(sel_b, sv[lt][1])) + pad_ref[...]
            cidx = _select_rows(sel_a, si[lt][0]) * float(N_KEYS) + _select_rows(sel_b, si[lt][1])
            top_s, idx = _topk_rows(cand, PEER_TOPK, payload=cidx)
            e = jnp.exp(top_s - top_s[0:1])
            gate_rows[lt].append(e / jnp.sum(e, axis=0, keepdims=True))
            row_list[lt].append(idx)
    for lt in range(n_tiles):
        tok = slice(lt * LANES, (lt + 1) * LANES)
        rows_ref[tok, :] = (jnp.concatenate(row_list[lt], axis=0).astype(jnp.int32) * ROW_WORDS).T
        gate_ref[tok, :] = jnp.concatenate(gate_rows[lt], axis=0).T


def _peer_route(x1, g_ffn, w_query, sub_keys, *, tb):
    n, d = x1.shape
    nk = PEER_HEADS * PEER_TOPK
    keys = sub_keys.reshape(PEER_HEADS * 2, N_KEYS, PEER_HALF).astype(BF16)
    sel_a, sel_b, pad = _pair_selectors()
    row = lambda w: pl.BlockSpec((tb, w), lambda i: (i, 0))
    return pl.pallas_call(
        _peer_route_kernel,
        grid=(n // tb,),
        in_specs=[row(d), _const_spec((1, d)), _const_spec(w_query.shape), _const_spec(keys.shape),
                  _const_spec(sel_a.shape), _const_spec(sel_b.shape), _const_spec(pad.shape)],
        out_specs=(row(nk), row(nk)),
        out_shape=(jax.ShapeDtypeStruct((n, nk), jnp.int32), jax.ShapeDtypeStruct((n, nk), F32)),
        compiler_params=_cparams(1),
        name="peer_route",
    )(x1, g_ffn[None, :], w_query.astype(BF16), keys, sel_a, sel_b, pad)


ROW_WORDS = 4


def _pack_table(tbl):
    e, d = tbl.shape
    bits = lax.bitcast_convert_type(tbl.astype(BF16), jnp.uint16).astype(jnp.uint32)
    half = d // 2
    words = bits[:, :half] | (bits[:, half:] << 16)
    return lax.bitcast_convert_type(words, jnp.int32).reshape(e * ROW_WORDS, LANES)


def _unpack(w):
    lo = lax.bitcast_convert_type(w << 16, F32)
    hi = lax.bitcast_convert_type(w & jnp.int32(-65536), F32)
    return lo, hi


def _resident_spec(shape):
    nd = len(shape)
    return pl.BlockSpec(shape, lambda *_: (0,) * nd, pipeline_mode=pl.Buffered(1))


def _smem_rows_spec(tb, nk):
    return pl.BlockSpec((tb, nk), lambda i: (i, 0), memory_space=pltpu.SMEM)


def _peer_up_kernel(rows_ref, x_ref, g_ref, gate_ref, ones_ref, eye_ref, tbl, hid_ref, h_s, s_s, a_s):
    x = x_ref[...]
    ss = jnp.sum(jnp.sum(x * x, axis=2, keepdims=True), axis=1, keepdims=True)
    h_s[...] = (x * lax.rsqrt(ss * (1.0 / (SUBLANES * LANES)) + EPS)) * g_ref[...]
    tb = x.shape[0]
    nk = rows_ref.shape[1]

    def token(c, carry):
        xv = h_s[c]
        xl = xv[:ROW_WORDS]
        xh = xv[ROW_WORDS:]
        for r in range(nk):
            row = pl.multiple_of(rows_ref[c, r], ROW_WORDS)
            lo, hi = _unpack(tbl[pl.ds(row, ROW_WORDS), :])
            s_s[r * ROW_WORDS:(r + 1) * ROW_WORDS, :] = lo * xl + hi * xh
        part = s_s[pl.ds(0, nk, stride=ROW_WORDS), :]
        for j in range(1, ROW_WORDS):
            part = part + s_s[pl.ds(j, nk, stride=ROW_WORDS), :]
        a_s[c] = part
        return carry

    lax.fori_loop(0, tb, token, 0)
    ones = ones_ref[...]
    tc = min(tb, 16)
    for c0 in range(0, tb, tc):
        part = a_s[c0:c0 + tc].reshape(tc * nk, LANES)
        p_hi = part.astype(BF16)
        p_lo = (part - p_hi.astype(F32)).astype(BF16)
        sums = (jnp.dot(p_hi, ones, preferred_element_type=F32)
                + jnp.dot(p_lo, ones, preferred_element_type=F32)).reshape(tc, nk, LANES)
        a = jnp.sum(sums * eye_ref[...], axis=1)
        hid_ref[c0:c0 + tc, :] = (0.5 * a * (1.0 + lax.erf(a * (2.0 ** -0.5)))) * gate_ref[c0:c0 + tc, :]


def _peer_up(x1r, g_ffn, rows, gate, tbl_u, *, tb):
    n = x1r.shape[0]
    nk = rows.shape[1]
    assert nk == LANES
    return pl.pallas_call(
        _peer_up_kernel,
        grid=(n // tb,),
        in_specs=[_smem_rows_spec(tb, nk),
                  pl.BlockSpec((tb, SUBLANES, LANES), lambda i: (i, 0, 0)),
                  _const_spec((SUBLANES, LANES)),
                  pl.BlockSpec((tb, nk), lambda i: (i, 0)),
                  _const_spec((LANES, LANES)), _const_spec((nk, LANES)),
                  _resident_spec(tbl_u.shape)],
        out_specs=pl.BlockSpec((tb, nk), lambda i: (i, 0)),
        out_shape=jax.ShapeDtypeStruct((n, nk), F32),
        scratch_shapes=[pltpu.VMEM((tb, SUBLANES, LANES), F32),
                        pltpu.VMEM((nk * ROW_WORDS, LANES), F32),
                        pltpu.VMEM((tb, nk, LANES), F32)],
        compiler_params=_cparams(1),
        name="peer_up",
    )(rows, x1r, g_ffn.reshape(SUBLANES, LANES), gate, jnp.ones((LANES, LANES), BF16),
      jnp.eye(nk, LANES, dtype=F32), tbl_u)


def _peer_down_kernel(rows_ref, hid_ref, x_ref, g_ref, tbl, y_ref, w_s, o_s):
    tb = x_ref.shape[0]
    nk = rows_ref.shape[1]
    n_acc = 4
    hid_t = hid_ref[...].T
    for c in range(tb):
        w_s[c] = jnp.broadcast_to(hid_t[:, c:c + 1], (nk, LANES))

    def token(c, carry):
        acc_lo = [jnp.zeros((ROW_WORDS, LANES), F32) for _ in range(n_acc)]
        acc_hi = [jnp.zeros((ROW_WORDS, LANES), F32) for _ in range(n_acc)]
        for r in range(nk):
            row = pl.multiple_of(rows_ref[c, r], ROW_WORDS)
            lo, hi = _unpack(tbl[pl.ds(row, ROW_WORDS), :])
            wgt = jnp.broadcast_to(w_s[c, r:r + 1, :], (ROW_WORDS, LANES))
            acc_lo[r % n_acc] = acc_lo[r % n_acc] + wgt * lo
            acc_hi[r % n_acc] = acc_hi[r % n_acc] + wgt * hi
        lo = (acc_lo[0] + acc_lo[1]) + (acc_lo[2] + acc_lo[3])
        hi = (acc_hi[0] + acc_hi[1]) + (acc_hi[2] + acc_hi[3])
        o_s[c] = x_ref[c] + jnp.concatenate([lo, hi], axis=0)
        return carry

    lax.fori_loop(0, tb, token, 0)
    xo = o_s[...]
    ss = jnp.sum(jnp.sum(xo * xo, axis=2, keepdims=True), axis=1, keepdims=True)
    y_ref[...] = (xo * lax.rsqrt(ss * (1.0 / (SUBLANES * LANES)) + EPS)) * g_ref[...]


def _peer_down(x1r, g_final, rows, hid, tbl_v, *, tb):
    n = x1r.shape[0]
    nk = rows.shape[1]
    tok = pl.BlockSpec((tb, SUBLANES, LANES), lambda i: (i, 0, 0))
    return pl.pallas_call(
        _peer_down_kernel,
        grid=(n // tb,),
        in_specs=[_smem_rows_spec(tb, nk), pl.BlockSpec((tb, nk), lambda i: (i, 0)), tok,
                  _const_spec((SUBLANES, LANES)), _resident_spec(tbl_v.shape)],
        out_specs=tok,
        out_shape=jax.ShapeDtypeStruct((n, SUBLANES, LANES), F32),
        scratch_shapes=[pltpu.VMEM((tb, nk, LANES), F32),
                        pltpu.VMEM((tb, SUBLANES, LANES), F32)],
        compiler_params=_cparams(1),
        name="peer_down",
    )(rows, hid, x1r, g_final.reshape(SUBLANES, LANES), tbl_v)


def _pick(n, pref):
    t = min(pref, n)
    while n % t:
        t //= 2
    return t


def _trunk(x, g_mix, w_in, pool_w, pool_scale, q_norm, k_norm, w_pool_out, w_attn_out, w_out,
           g_ffn, w_query, sub_keys, tbl_u, tbl_v, g_final):
    b, t, d = x.shape
    n = b * t
    x2d = x.reshape(n, d)
    tm = _pick(t, 512)
    p, q, k0, k1, v0, v1, gp, ga = _in_proj(x2d, t, g_mix, w_in, q_norm, k_norm, tm=tm)
    r3 = lambda a: a.reshape(b, t, a.shape[-1])
    attn = _attention(r3(q), r3(k0), r3(k1), r3(v0), r3(v1), tq=_pick(t, 128))
    x1 = _merge(x2d, t, p, attn.reshape(n, -1), gp, ga, pool_w, pool_scale, w_pool_out, w_attn_out,
                w_out, tm=tm)
    rows, gate = _peer_route(x1, g_ffn, w_query, sub_keys, tb=_pick(n, 256))
    x1r = x1.reshape(n, SUBLANES, LANES)
    tbp = _pick(n, 128)
    hid = _peer_up(x1r, g_ffn, rows, gate, tbl_u, tb=tbp)
    y = _peer_down(x1r, g_final, rows, hid, tbl_v, tb=tbp)
    return y.reshape(b, t, d)


def kernel(x_prompt, x_sample, g_mix, w_in, pool_w, pool_scale, q_norm, k_norm, w_pool_out, w_attn_out,
           w_out, g_ffn, w_query, sub_keys, expert_u, expert_v, g_final):
    assert g_mix.shape[0] == 1, "single-layer trunk"
    tbl_u = _pack_table(expert_u[0])
    tbl_v = _pack_table(expert_v[0])
    args = (g_mix[0], w_in[0], pool_w[0], pool_scale[0], q_norm[0], k_norm[0], w_pool_out[0],
            w_attn_out[0], w_out[0], g_ffn[0], w_query[0], sub_keys[0], tbl_u, tbl_v, g_final)
    return (_trunk(x_prompt, *args), _trunk(x_sample, *args))
```

```python
import functools
import math

import jax
import jax.numpy as jnp
from jax import lax
from jax.experimental import pallas as pl
from jax.experimental.pallas import tpu as pltpu

GRID_W = 64
POOL_WINDOWS = (2, 4, 8, 16)
POOL_GROUP = 128
N_HEADS = 8
N_KV_HEADS = 2
HEAD_DIM = 64
ROPE_THETA = 10000.0
N_KEYS = 128
PEER_HEADS = 8
PEER_HALF = 128
PEER_TOPK = 16
EPS = 1e-6

LANES = 128
SUBLANES = 8
VMEM_LIMIT_BYTES = 56 * 1024 * 1024

POOL_HALO = 16

F32 = jnp.float32
BF16 = jnp.bfloat16
NEG_INF = float("-inf")


def _cparams(n_axes):
    return pltpu.CompilerParams(
        dimension_semantics=("arbitrary",) * n_axes,
        vmem_limit_bytes=VMEM_LIMIT_BYTES,
    )


def _const_spec(shape):
    nd = len(shape)
    return pl.BlockSpec(shape, lambda *_: (0,) * nd)


def _in_proj_kernel(x_ref, g_ref, w_ref, qg_ref, qgp_ref, kg_ref, kgp_ref, cos_ref, sin_ref,
                    p_ref, q_ref, k0_ref, k1_ref, v0_ref, v1_ref, gp_ref, ga_ref, *, widths):
    x = x_ref[...]
    r = lax.rsqrt(jnp.mean(x * x, axis=-1, keepdims=True) + EPS)
    h = ((x * r) * g_ref[...]).astype(BF16)

    offs = {}
    o = 0
    for name, w in widths:
        offs[name] = (o, w)
        o += w

    def proj(name):
        a, w = offs[name]
        return jnp.dot(h, w_ref[:, a:a + w], preferred_element_type=F32)

    p_ref[...] = proj("p")

    cos = cos_ref[...]
    sin = sin_ref[...]
    lane = lax.broadcasted_iota(jnp.int32, (1, LANES), 1)
    first_head = lane < HEAD_DIM

    def norm_rope(z, zp, gain, gain_p):
        sq = z * z
        ss_a = jnp.sum(jnp.where(first_head, sq, 0.0), axis=-1, keepdims=True)
        ss_b = jnp.sum(jnp.where(first_head, 0.0, sq), axis=-1, keepdims=True)
        rr = lax.rsqrt(jnp.where(first_head, ss_a, ss_b) * (1.0 / HEAD_DIM) + EPS)
        return ((z * rr) * gain) * cos + ((zp * rr) * gain_p) * sin

    zq = proj("q")
    zqp = proj("qp")
    scale = HEAD_DIM ** -0.5 * math.log2(math.e)
    for c in range(zq.shape[1] // LANES):
        sl = slice(c * LANES, (c + 1) * LANES)
        qr = norm_rope(zq[:, sl], zqp[:, sl], qg_ref[...], qgp_ref[...])
        q_ref[:, sl] = (qr * scale).astype(BF16)

    kr = norm_rope(proj("k"), proj("kp"), kg_ref[...], kgp_ref[...]).astype(BF16)
    k0_ref[...] = kr[:, :HEAD_DIM]
    k1_ref[...] = kr[:, HEAD_DIM:]
    zv = proj("v").astype(BF16)
    v0_ref[...] = zv[:, :HEAD_DIM]
    v1_ref[...] = zv[:, HEAD_DIM:]
    gp_ref[...] = jax.nn.sigmoid(proj("gp")).astype(BF16)
    ga_ref[...] = jax.nn.sigmoid(proj("ga")).astype(BF16)


def _rope_partner_perm(n_heads):
    q = HEAD_DIM // 4
    base = jnp.concatenate([jnp.arange(q, 2 * q), jnp.arange(0, q),
                            jnp.arange(3 * q, 4 * q), jnp.arange(2 * q, 3 * q)])
    return (jnp.arange(n_heads)[:, None] * HEAD_DIM + base[None, :]).reshape(-1)


def _rope_tables(seq):
    quarter = HEAD_DIM // 4
    inv_freq = ROPE_THETA ** (-jnp.arange(quarter, dtype=F32) / quarter)
    t = jnp.arange(seq)
    ang_r = (t // GRID_W).astype(F32)[:, None] * inv_freq
    ang_c = (t % GRID_W).astype(F32)[:, None] * inv_freq
    cos = jnp.concatenate([jnp.cos(ang_r)] * 2 + [jnp.cos(ang_c)] * 2, axis=-1)
    sin = jnp.concatenate([-jnp.sin(ang_r), jnp.sin(ang_r), -jnp.sin(ang_c), jnp.sin(ang_c)], axis=-1)
    return jnp.tile(cos, (1, 2)), jnp.tile(sin, (1, 2))


def _in_proj(x2d, seq, g_mix, w_in, q_norm, k_norm, *, tm):
    n, d = x2d.shape
    pool_w = POOL_GROUP * len(POOL_WINDOWS)
    q_w = N_HEADS * HEAD_DIM
    kv_w = N_KV_HEADS * HEAD_DIM
    o1 = pool_w
    o2 = o1 + q_w
    o3 = o2 + kv_w
    o4 = o3 + kv_w
    o5 = o4 + d
    wq = w_in[:, o1:o2]
    wk = w_in[:, o2:o3]
    w_cat = jnp.concatenate(
        [w_in[:, :o1], wq, wq[:, _rope_partner_perm(N_HEADS)], wk, wk[:, _rope_partner_perm(N_KV_HEADS)],
         w_in[:, o3:o4], w_in[:, o4:o5], w_in[:, o5:]], axis=1).astype(BF16)
    widths = (("p", pool_w), ("q", q_w), ("qp", q_w), ("k", kv_w), ("kp", kv_w), ("v", kv_w),
              ("gp", d), ("ga", d))
    perm1 = _rope_partner_perm(1)
    qg = jnp.tile(q_norm, 2)[None, :]
    qgp = jnp.tile(q_norm[perm1], 2)[None, :]
    kg = jnp.tile(k_norm, 2)[None, :]
    kgp = jnp.tile(k_norm[perm1], 2)[None, :]
    cos, sin = _rope_tables(seq)
    nb_seq = seq // tm

    row = lambda w: pl.BlockSpec((tm, w), lambda i: (i, 0))
    tab = pl.BlockSpec((tm, LANES), lambda i: (i % nb_seq, 0))
    out_shape = (
        jax.ShapeDtypeStruct((n, pool_w), F32),
        jax.ShapeDtypeStruct((n, q_w), BF16),
        jax.ShapeDtypeStruct((n, HEAD_DIM), BF16), jax.ShapeDtypeStruct((n, HEAD_DIM), BF16),
        jax.ShapeDtypeStruct((n, HEAD_DIM), BF16), jax.ShapeDtypeStruct((n, HEAD_DIM), BF16),
        jax.ShapeDtypeStruct((n, d), BF16), jax.ShapeDtypeStruct((n, d), BF16),
    )
    return pl.pallas_call(
        functools.partial(_in_proj_kernel, widths=widths),
        grid=(n // tm,),
        in_specs=[row(d), _const_spec((1, d)), _const_spec(w_cat.shape),
                  _const_spec((1, LANES)), _const_spec((1, LANES)), _const_spec((1, LANES)),
                  _const_spec((1, LANES)), tab, tab],
        out_specs=(row(pool_w), row(q_w), row(HEAD_DIM), row(HEAD_DIM), row(HEAD_DIM), row(HEAD_DIM),
                   row(d), row(d)),
        out_shape=out_shape,
        compiler_params=_cparams(1),
        name="in_proj",
    )(x2d, g_mix[None, :], w_cat, qg, qgp, kg, kgp, cos, sin)


def _attn_kernel(q_ref, k0_ref, k1_ref, v0_ref, v1_ref, o_ref, *, ck):
    grp = N_HEADS // N_KV_HEADS
    q = q_ref[0]
    tq = q.shape[0]
    n_chunks = k0_ref.shape[1] // ck
    for g, (k_ref, v_ref) in enumerate(((k0_ref, v0_ref), (k1_ref, v1_ref))):
        qs = jnp.concatenate(
            [q[:, (g * grp + j) * HEAD_DIM:(g * grp + j + 1) * HEAD_DIM] for j in range(grp)], axis=0)

        def scores(j):
            return lax.dot_general(qs, k_ref[0, j * ck:(j + 1) * ck, :], (((1,), (1,)), ((), ())),
                                   preferred_element_type=F32)

        m = jnp.max(scores(0), axis=-1, keepdims=True)
        for j in range(1, n_chunks):
            m = jnp.maximum(m, jnp.max(scores(j), axis=-1, keepdims=True))
        l = jnp.zeros_like(m)
        acc = jnp.zeros((qs.shape[0], HEAD_DIM), F32)
        for j in range(n_chunks):
            e = jnp.exp2(scores(j) - m)
            l = l + jnp.sum(e, axis=-1, keepdims=True)
            acc = acc + jnp.dot(e.astype(BF16), v_ref[0, j * ck:(j + 1) * ck, :], preferred_element_type=F32)
        o = (acc / l).astype(BF16)
        for j in range(grp):
            h = g * grp + j
            o_ref[0, :, h * HEAD_DIM:(h + 1) * HEAD_DIM] = o[j * tq:(j + 1) * tq]


def _attention(q, k0, k1, v0, v1, *, tq):
    b, t, qw = q.shape
    kv = pl.BlockSpec((1, t, HEAD_DIM), lambda bi, i: (bi, 0, 0))
    qs = pl.BlockSpec((1, tq, qw), lambda bi, i: (bi, i, 0))
    return pl.pallas_call(
        functools.partial(_attn_kernel, ck=_pick(t, 512)),
        grid=(b, t // tq),
        in_specs=[qs, kv, kv, kv, kv],
        out_specs=qs,
        out_shape=jax.ShapeDtypeStruct((b, t, qw), BF16),
        compiler_params=_cparams(2),
        name="attention",
    )(q, k0, k1, v0, v1)


def _merge_kernel(x_ref, p_ref, prev_ref, next_ref, attn_ref, gp_ref, ga_ref, pw_ref, ps_ref,
                  wpo_ref, wao_ref, wo_ref, o_ref, e_ref, *, nb_seq, seq):
    i = pl.program_id(0)
    tm = x_ref.shape[0]
    blk = i % nb_seq
    halo = POOL_HALO
    e_ref[0:halo, :] = jnp.where(blk == 0, 0.0, prev_ref[...])
    e_ref[halo:halo + tm, :] = p_ref[...]
    e_ref[halo + tm:, :] = jnp.where(blk == nb_seq - 1, 0.0, next_ref[...])

    t = blk * tm + lax.broadcasted_iota(jnp.int32, (tm, 1), 0)
    mixed = []
    for gi, w in enumerate(POOL_WINDOWS):
        cols = slice(gi * POOL_GROUP, (gi + 1) * POOL_GROUP)
        acc = None
        for dlt in range(-(w // 2), w - w // 2):
            piece = e_ref[halo + dlt:halo + dlt + tm, cols]
            acc = piece if acc is None else acc + piece
        cnt = (jnp.minimum(t + (w - w // 2), seq) - jnp.maximum(t - w // 2, 0)).astype(F32)
        pooled = (acc / cnt - e_ref[halo:halo + tm, cols]).astype(BF16)
        mixed.append(jnp.dot(pooled, pw_ref[gi], preferred_element_type=F32) * ps_ref[:, cols])
    mixed = jnp.concatenate(mixed, axis=-1).astype(BF16)
    pool_branch = jnp.dot(mixed, wpo_ref[...], preferred_element_type=F32)
    attn_branch = jnp.dot(attn_ref[...], wao_ref[...], preferred_element_type=F32)
    merged = gp_ref[...].astype(F32) * pool_branch + ga_ref[...].astype(F32) * attn_branch
    o_ref[...] = x_ref[...] + jnp.dot(merged.astype(BF16), wo_ref[...], preferred_element_type=F32)


def _merge(x2d, seq, p, attn, gp, ga, pool_w, pool_scale, w_pool_out, w_attn_out, w_out, *, tm):
    n, d = x2d.shape
    pw = p.shape[1]
    nb_seq = seq // tm
    hb = tm // POOL_HALO
    n_halo_blocks = n // POOL_HALO
    row = lambda w: pl.BlockSpec((tm, w), lambda i: (i, 0))
    prev = pl.BlockSpec((POOL_HALO, pw), lambda i: (jnp.maximum(i * hb - 1, 0), 0))
    nxt = pl.BlockSpec((POOL_HALO, pw), lambda i: (jnp.minimum((i + 1) * hb, n_halo_blocks - 1), 0))
    return pl.pallas_call(
        functools.partial(_merge_kernel, nb_seq=nb_seq, seq=seq),
        grid=(n // tm,),
        in_specs=[row(d), row(pw), prev, nxt, row(attn.shape[1]), row(d), row(d),
                  _const_spec(pool_w.shape), _const_spec((1, pw)), _const_spec(w_pool_out.shape),
                  _const_spec(w_attn_out.shape), _const_spec(w_out.shape)],
        out_specs=row(d),
        out_shape=jax.ShapeDtypeStruct((n, d), F32),
        scratch_shapes=[pltpu.VMEM((tm + 2 * POOL_HALO, pw), F32)],
        compiler_params=_cparams(1),
        name="merge",
    )(x2d, p, p, p, attn, gp, ga, pool_w.astype(BF16), pool_scale[None, :], w_pool_out.astype(BF16),
      w_attn_out.astype(BF16), w_out.astype(BF16))


def _topk_rows(vals, k, payload=None):
    n = vals.shape[0]
    iota = lax.broadcasted_iota(jnp.int32, vals.shape, 0).astype(F32)
    cur = vals
    out_v, out_p = [], []
    for _ in range(k):
        m = jnp.max(cur, axis=0, keepdims=True)
        pos = jnp.min(jnp.where(cur == m, iota, float(n)), axis=0, keepdims=True)
        hit = iota == pos
        out_v.append(m)
        if payload is None:
            out_p.append(pos)
        else:
            out_p.append(jnp.sum(jnp.where(hit, payload, 0.0), axis=0, keepdims=True))
        cur = jnp.where(hit, NEG_INF, cur)
    return jnp.concatenate(out_v, axis=0), jnp.concatenate(out_p, axis=0)


def _candidate_pairs():
    return [(a, b) for a in range(PEER_TOPK) for b in range(PEER_TOPK) if (a + 1) * (b + 1) <= PEER_TOPK]


def _pair_selectors():
    pairs = _candidate_pairs()
    n = len(pairs) + (-len(pairs) % SUBLANES)
    rows = jnp.arange(len(pairs))
    sel_a = jnp.zeros((n, PEER_TOPK), F32).at[rows, jnp.array([p[0] for p in pairs])].set(1.0)
    sel_b = jnp.zeros((n, PEER_TOPK), F32).at[rows, jnp.array([p[1] for p in pairs])].set(1.0)
    pad = jnp.where(jnp.arange(n) < len(pairs), 0.0, NEG_INF).astype(F32)[:, None]
    return sel_a, sel_b, pad


def _select_rows(sel, x):
    return jnp.dot(sel, x, precision=lax.Precision.HIGHEST, preferred_element_type=F32)


def _peer_route_kernel(x_ref, g_ref, wq_ref, keys_ref, sela_ref, selb_ref, pad_ref, rows_ref, gate_ref):
    x = x_ref[...]
    r = lax.rsqrt(jnp.mean(x * x, axis=-1, keepdims=True) + EPS)
    h = ((x * r) * g_ref[...]).astype(BF16)
    sel_a = sela_ref[...]
    sel_b = selb_ref[...]
    n_tiles = x.shape[0] // LANES
    row_list = [[] for _ in range(n_tiles)]
    gate_rows = [[] for _ in range(n_tiles)]
    for hd in range(PEER_HEADS):
        sv = [[] for _ in range(n_tiles)]
        si = [[] for _ in range(n_tiles)]
        for part in range(2):
            c0 = (hd * 2 + part) * PEER_HALF
            qh = jnp.dot(h, wq_ref[:, c0:c0 + PEER_HALF], preferred_element_type=F32).astype(BF16)
            s_t = lax.dot_general(keys_ref[hd * 2 + part], qh, (((1,), (1,)), ((), ())),
                                  preferred_element_type=F32)
            for lt in range(n_tiles):
                v, ix = _topk_rows(s_t[:, lt * LANES:(lt + 1) * LANES], PEER_TOPK)
                sv[lt].append(v)
                si[lt].append(ix)
        for lt in range(n_tiles):
            cand = (_select_rows(sel_a, sv[lt][0]) + _select_rows(sel_b, sv[lt][1])) + pad_ref[...]
            cidx = _select_rows(sel_a, si[lt][0]) * float(N_KEYS) + _select_rows(sel_b, si[lt][1])
            top_s, idx = _topk_rows(cand, PEER_TOPK, payload=cidx)
            e = jnp.exp(top_s - top_s[0:1])
            gate_rows[lt].append(e / jnp.sum(e, axis=0, keepdims=True))
            row_list[lt].append(idx)
    for lt in range(n_tiles):
        tok = slice(lt * LANES, (lt + 1) * LANES)
        rows_ref[tok, :] = (jnp.concatenate(row_list[lt], axis=0).astype(jnp.int32) * ROW_WORDS).T
        gate_ref[tok, :] = jnp.concatenate(gate_rows[lt], axis=0).T


def _peer_route(x1, g_ffn, w_query, sub_keys, *, tb):
    n, d = x1.shape
    nk = PEER_HEADS * PEER_TOPK
    keys = sub_keys.reshape(PEER_HEADS * 2, N_KEYS, PEER_HALF).astype(BF16)
    sel_a, sel_b, pad = _pair_selectors()
    row = lambda w: pl.BlockSpec((tb, w), lambda i: (i, 0))
    return pl.pallas_call(
        _peer_route_kernel,
        grid=(n // tb,),
        in_specs=[row(d), _const_spec((1, d)), _const_spec(w_query.shape), _const_spec(keys.shape),
                  _const_spec(sel_a.shape), _const_spec(sel_b.shape), _const_spec(pad.shape)],
        out_specs=(row(nk), row(nk)),
        out_shape=(jax.ShapeDtypeStruct((n, nk), jnp.int32), jax.ShapeDtypeStruct((n, nk), F32)),
        compiler_params=_cparams(1),
        name="peer_route",
    )(x1, g_ffn[None, :], w_query.astype(BF16), keys, sel_a, sel_b, pad)


ROW_WORDS = 4


def _pack_table(tbl):
    e, d = tbl.shape
    bits = lax.bitcast_convert_type(tbl.astype(BF16), jnp.uint16).astype(jnp.uint32)
    half = d // 2
    words = bits[:, :half] | (bits[:, half:] << 16)
    return lax.bitcast_convert_type(words, jnp.int32).reshape(e * ROW_WORDS, LANES)


def _unpack(w):
    lo = lax.bitcast_convert_type(w << 16, F32)
    hi = lax.bitcast_convert_type(w & jnp.int32(-65536), F32)
    return lo, hi


def _resident_spec(shape):
    nd = len(shape)
    return pl.BlockSpec(shape, lambda *_: (0,) * nd, pipeline_mode=pl.Buffered(1))


ROW_GROUP = 8


def _row_offset_scratch(nk):
    return [pltpu.SMEM((2, ROW_GROUP, nk), jnp.int32), pltpu.SemaphoreType.DMA((2,))]


def _for_each_token(rows_vmem, idx_smem, sem, token):
    n_groups = rows_vmem.shape[0] // ROW_GROUP

    def fetch(group, slot):
        return pltpu.make_async_copy(rows_vmem.at[pl.ds(group * ROW_GROUP, ROW_GROUP)], idx_smem.at[slot],
                                     sem.at[slot])

    def run_group(group, slot):
        for u in range(ROW_GROUP):
            token(group * ROW_GROUP + u, lambda r, u=u: idx_smem[slot, u, r])

    fetch(0, 0).start()

    def trip(j, carry):
        g0 = 2 * j
        fetch(g0 + 1, 1).start()
        fetch(g0, 0).wait()
        run_group(g0, 0)
        ahead = jnp.minimum(g0 + 2, n_groups - 1)
        fetch(ahead, 0).start()
        fetch(g0 + 1, 1).wait()
        run_group(g0 + 1, 1)
        return carry

    lax.fori_loop(0, n_groups // 2, trip, 0)
    fetch(n_groups - 1, 0).wait()


def _peer_up_kernel(rows_ref, x_ref, g_ref, gate_ref, ones_ref, eye_ref, tbl, hid_ref,
                    h_s, s_s, a_s, idx_smem, sem):
    x = x_ref[...]
    ss = jnp.sum(jnp.sum(x * x, axis=2, keepdims=True), axis=1, keepdims=True)
    h_s[...] = (x * lax.rsqrt(ss * (1.0 / (SUBLANES * LANES)) + EPS)) * g_ref[...]
    tb = x.shape[0]
    nk = rows_ref.shape[1]

    def token(c, row_of):
        xv = h_s[c]
        xl = xv[:ROW_WORDS]
        xh = xv[ROW_WORDS:]
        for r in range(nk):
            row = pl.multiple_of(row_of(r), ROW_WORDS)
            lo, hi = _unpack(tbl[pl.ds(row, ROW_WORDS), :])
            s_s[r * ROW_WORDS:(r + 1) * ROW_WORDS, :] = lo * xl + hi * xh
        part = s_s[pl.ds(0, nk, stride=ROW_WORDS), :]
        for j in range(1, ROW_WORDS):
            part = part + s_s[pl.ds(j, nk, stride=ROW_WORDS), :]
        a_s[c] = part

    _for_each_token(rows_ref, idx_smem, sem, token)
    ones = ones_ref[...]
    tc = min(tb, 16)
    for c0 in range(0, tb, tc):
        part = a_s[c0:c0 + tc].reshape(tc * nk, LANES)
        p_hi = part.astype(BF16)
        p_lo = (part - p_hi.astype(F32)).astype(BF16)
        sums = (jnp.dot(p_hi, ones, preferred_element_type=F32)
                + jnp.dot(p_lo, ones, preferred_element_type=F32)).reshape(tc, nk, LANES)
        a = jnp.sum(sums * eye_ref[...], axis=1)
        hid_ref[c0:c0 + tc, :] = (0.5 * a * (1.0 + lax.erf(a * (2.0 ** -0.5)))) * gate_ref[c0:c0 + tc, :]


def _peer_up(x1r, g_ffn, rows, gate, tbl_u, *, tb):
    n = x1r.shape[0]
    nk = rows.shape[1]
    assert nk == LANES and tb % (2 * ROW_GROUP) == 0
    return pl.pallas_call(
        _peer_up_kernel,
        grid=(n // tb,),
        in_specs=[pl.BlockSpec((tb, nk), lambda i: (i, 0)),
                  pl.BlockSpec((tb, SUBLANES, LANES), lambda i: (i, 0, 0)),
                  _const_spec((SUBLANES, LANES)),
                  pl.BlockSpec((tb, nk), lambda i: (i, 0)),
                  _const_spec((LANES, LANES)), _const_spec((nk, LANES)),
                  _resident_spec(tbl_u.shape)],
        out_specs=pl.BlockSpec((tb, nk), lambda i: (i, 0)),
        out_shape=jax.ShapeDtypeStruct((n, nk), F32),
        scratch_shapes=[pltpu.VMEM((tb, SUBLANES, LANES), F32),
                        pltpu.VMEM((nk * ROW_WORDS, LANES), F32),
                        pltpu.VMEM((tb, nk, LANES), F32)] + _row_offset_scratch(nk),
        compiler_params=_cparams(1),
        name="peer_up",
    )(rows, x1r, g_ffn.reshape(SUBLANES, LANES), gate, jnp.ones((LANES, LANES), BF16),
      jnp.eye(nk, LANES, dtype=F32), tbl_u)


def _peer_down_kernel(rows_ref, hid_ref, x_ref, g_ref, tbl, y_ref, w_s, o_s, idx_smem, sem):
    tb = x_ref.shape[0]
    nk = rows_ref.shape[1]
    n_acc = 4
    hid_t = hid_ref[...].T
    for c in range(tb):
        w_s[c] = jnp.broadcast_to(hid_t[:, c:c + 1], (nk, LANES))

    def token(c, row_of):
        acc_lo = [jnp.zeros((ROW_WORDS, LANES), F32) for _ in range(n_acc)]
        acc_hi = [jnp.zeros((ROW_WORDS, LANES), F32) for _ in range(n_acc)]
        for r in range(nk):
            row = pl.multiple_of(row_of(r), ROW_WORDS)
            lo, hi = _unpack(tbl[pl.ds(row, ROW_WORDS), :])
            wgt = jnp.broadcast_to(w_s[c, r:r + 1, :], (ROW_WORDS, LANES))
            acc_lo[r % n_acc] = acc_lo[r % n_acc] + wgt * lo
            acc_hi[r % n_acc] = acc_hi[r % n_acc] + wgt * hi
        lo = (acc_lo[0] + acc_lo[1]) + (acc_lo[2] + acc_lo[3])
        hi = (acc_hi[0] + acc_hi[1]) + (acc_hi[2] + acc_hi[3])
        o_s[c] = x_ref[c] + jnp.concatenate([lo, hi], axis=0)

    _for_each_token(rows_ref, idx_smem, sem, token)
    xo = o_s[...]
    ss = jnp.sum(jnp.sum(xo * xo, axis=2, keepdims=True), axis=1, keepdims=True)
    y_ref[...] = (xo * lax.rsqrt(ss * (1.0 / (SUBLANES * LANES)) + EPS)) * g_ref[...]


def _peer_down(x1r, g_final, rows, hid, tbl_v, *, tb):
    n = x1r.shape[0]
    nk = rows.shape[1]
    tok = pl.BlockSpec((tb, SUBLANES, LANES), lambda i: (i, 0, 0))
    return pl.pallas_call(
        _peer_down_kernel,
        grid=(n // tb,),
        in_specs=[pl.BlockSpec((tb, nk), lambda i: (i, 0)), pl.BlockSpec((tb, nk), lambda i: (i, 0)), tok,
                  _const_spec((SUBLANES, LANES)), _resident_spec(tbl_v.shape)],
        out_specs=tok,
        out_shape=jax.ShapeDtypeStruct((n, SUBLANES, LANES), F32),
        scratch_shapes=[pltpu.VMEM((tb, nk, LANES), F32),
                        pltpu.VMEM((tb, SUBLANES, LANES), F32)] + _row_offset_scratch(nk),
        compiler_params=_cparams(1),
        name="peer_down",
    )(rows, hid, x1r, g_final.reshape(SUBLANES, LANES), tbl_v)


def _pick(n, pref):
    t = min(pref, n)
    while n % t:
        t //= 2
    return t


def _trunk(x, g_mix, w_in, pool_w, pool_scale, q_norm, k_norm, w_pool_out, w_attn_out, w_out,
           g_ffn, w_query, sub_keys, tbl_u, tbl_v, g_final):
    b, t, d = x.shape
    n = b * t
    x2d = x.reshape(n, d)
    tm = _pick(t, 512)
    p, q, k0, k1, v0, v1, gp, ga = _in_proj(x2d, t, g_mix, w_in, q_norm, k_norm, tm=tm)
    r3 = lambda a: a.reshape(b, t, a.shape[-1])
    attn = _attention(r3(q), r3(k0), r3(k1), r3(v0), r3(v1), tq=_pick(t, 128))
    x1 = _merge(x2d, t, p, attn.reshape(n, -1), gp, ga, pool_w, pool_scale, w_pool_out, w_attn_out,
                w_out, tm=tm)
    rows, gate = _peer_route(x1, g_ffn, w_query, sub_keys, tb=_pick(n, 256))
    x1r = x1.reshape(n, SUBLANES, LANES)
    tbp = _pick(n, 128)
    hid = _peer_up(x1r, g_ffn, rows, gate, tbl_u, tb=tbp)
    y = _peer_down(x1r, g_final, rows, hid, tbl_v, tb=tbp)
    return y.reshape(b, t, d)


def kernel(x_prompt, x_sample, g_mix, w_in, pool_w, pool_scale, q_norm, k_norm, w_pool_out, w_attn_out,
           w_out, g_ffn, w_query, sub_keys, expert_u, expert_v, g_final):
    assert g_mix.shape[0] == 1, "single-layer trunk"
    tbl_u = _pack_table(expert_u[0])
    tbl_v = _pack_table(expert_v[0])
    args = (g_mix[0], w_in[0], pool_w[0], pool_scale[0], q_norm[0], k_norm[0], w_pool_out[0],
            w_attn_out[0], w_out[0], g_ffn[0], w_query[0], sub_keys[0], tbl_u, tbl_v, g_final)
    return (_trunk(x_prompt, *args), _trunk(x_sample, *args))
```

```python
import functools
import math

import jax
import jax.numpy as jnp
from jax import lax
from jax.experimental import pallas as pl
from jax.experimental.pallas import tpu as pltpu

GRID_W = 64
POOL_WINDOWS = (2, 4, 8, 16)
POOL_GROUP = 128
N_HEADS = 8
N_KV_HEADS = 2
HEAD_DIM = 64
ROPE_THETA = 10000.0
N_KEYS = 128
PEER_HEADS = 8
PEER_HALF = 128
PEER_TOPK = 16
EPS = 1e-6

LANES = 128
SUBLANES = 8
VMEM_LIMIT_BYTES = 56 * 1024 * 1024

POOL_HALO = 16

F32 = jnp.float32
BF16 = jnp.bfloat16
NEG_INF = float("-inf")


def _cparams(n_axes):
    return pltpu.CompilerParams(
        dimension_semantics=("arbitrary",) * n_axes,
        vmem_limit_bytes=VMEM_LIMIT_BYTES,
    )


def _const_spec(shape):
    nd = len(shape)
    return pl.BlockSpec(shape, lambda *_: (0,) * nd)


def _in_proj_kernel(x_ref, g_ref, w_ref, qg_ref, qgp_ref, kg_ref, kgp_ref, cos_ref, sin_ref,
                    p_ref, q_ref, k0_ref, k1_ref, v0_ref, v1_ref, gp_ref, ga_ref, *, widths):
    x = x_ref[...]
    r = lax.rsqrt(jnp.mean(x * x, axis=-1, keepdims=True) + EPS)
    h = ((x * r) * g_ref[...]).astype(BF16)

    offs = {}
    o = 0
    for name, w in widths:
        offs[name] = (o, w)
        o += w

    def proj(name):
        a, w = offs[name]
        return jnp.dot(h, w_ref[:, a:a + w], preferred_element_type=F32)

    p_ref[...] = proj("p")

    cos = cos_ref[...]
    sin = sin_ref[...]
    lane = lax.broadcasted_iota(jnp.int32, (1, LANES), 1)
    first_head = lane < HEAD_DIM

    def norm_rope(z, zp, gain, gain_p):
        sq = z * z
        ss_a = jnp.sum(jnp.where(first_head, sq, 0.0), axis=-1, keepdims=True)
        ss_b = jnp.sum(jnp.where(first_head, 0.0, sq), axis=-1, keepdims=True)
        rr = lax.rsqrt(jnp.where(first_head, ss_a, ss_b) * (1.0 / HEAD_DIM) + EPS)
        return ((z * rr) * gain) * cos + ((zp * rr) * gain_p) * sin

    zq = proj("q")
    zqp = proj("qp")
    scale = HEAD_DIM ** -0.5 * math.log2(math.e)
    for c in range(zq.shape[1] // LANES):
        sl = slice(c * LANES, (c + 1) * LANES)
        qr = norm_rope(zq[:, sl], zqp[:, sl], qg_ref[...], qgp_ref[...])
        q_ref[:, sl] = (qr * scale).astype(BF16)

    kr = norm_rope(proj("k"), proj("kp"), kg_ref[...], kgp_ref[...]).astype(BF16)
    k0_ref[...] = kr[:, :HEAD_DIM]
    k1_ref[...] = kr[:, HEAD_DIM:]
    zv = proj("v").astype(BF16)
    v0_ref[...] = zv[:, :HEAD_DIM]
    v1_ref[...] = zv[:, HEAD_DIM:]
    gp_ref[...] = jax.nn.sigmoid(proj("gp")).astype(BF16)
    ga_ref[...] = jax.nn.sigmoid(proj("ga")).astype(BF16)


def _rope_partner_perm(n_heads):
    q = HEAD_DIM // 4
    base = jnp.concatenate([jnp.arange(q, 2 * q), jnp.arange(0, q),
                            jnp.arange(3 * q, 4 * q), jnp.arange(2 * q, 3 * q)])
    return (jnp.arange(n_heads)[:, None] * HEAD_DIM + base[None, :]).reshape(-1)


def _rope_tables(seq):
    quarter = HEAD_DIM // 4
    inv_freq = ROPE_THETA ** (-jnp.arange(quarter, dtype=F32) / quarter)
    t = jnp.arange(seq)
    ang_r = (t // GRID_W).astype(F32)[:, None] * inv_freq
    ang_c = (t % GRID_W).astype(F32)[:, None] * inv_freq
    cos = jnp.concatenate([jnp.cos(ang_r)] * 2 + [jnp.cos(ang_c)] * 2, axis=-1)
    sin = jnp.concatenate([-jnp.sin(ang_r), jnp.sin(ang_r), -jnp.sin(ang_c), jnp.sin(ang_c)], axis=-1)
    return jnp.tile(cos, (1, 2)), jnp.tile(sin, (1, 2))


def _in_proj(x2d, seq, g_mix, w_in, q_norm, k_norm, *, tm):
    n, d = x2d.shape
    pool_w = POOL_GROUP * len(POOL_WINDOWS)
    q_w = N_HEADS * HEAD_DIM
    kv_w = N_KV_HEADS * HEAD_DIM
    o1 = pool_w
    o2 = o1 + q_w
    o3 = o2 + kv_w
    o4 = o3 + kv_w
    o5 = o4 + d
    wq = w_in[:, o1:o2]
    wk = w_in[:, o2:o3]
    w_cat = jnp.concatenate(
        [w_in[:, :o1], wq, wq[:, _rope_partner_perm(N_HEADS)], wk, wk[:, _rope_partner_perm(N_KV_HEADS)],
         w_in[:, o3:o4], w_in[:, o4:o5], w_in[:, o5:]], axis=1).astype(BF16)
    widths = (("p", pool_w), ("q", q_w), ("qp", q_w), ("k", kv_w), ("kp", kv_w), ("v", kv_w),
              ("gp", d), ("ga", d))
    perm1 = _rope_partner_perm(1)
    qg = jnp.tile(q_norm, 2)[None, :]
    qgp = jnp.tile(q_norm[perm1], 2)[None, :]
    kg = jnp.tile(k_norm, 2)[None, :]
    kgp = jnp.tile(k_norm[perm1], 2)[None, :]
    cos, sin = _rope_tables(seq)
    nb_seq = seq // tm

    row = lambda w: pl.BlockSpec((tm, w), lambda i: (i, 0))
    tab = pl.BlockSpec((tm, LANES), lambda i: (i % nb_seq, 0))
    out_shape = (
        jax.ShapeDtypeStruct((n, pool_w), F32),
        jax.ShapeDtypeStruct((n, q_w), BF16),
        jax.ShapeDtypeStruct((n, HEAD_DIM), BF16), jax.ShapeDtypeStruct((n, HEAD_DIM), BF16),
        jax.ShapeDtypeStruct((n, HEAD_DIM), BF16), jax.ShapeDtypeStruct((n, HEAD_DIM), BF16),
        jax.ShapeDtypeStruct((n, d), BF16), jax.ShapeDtypeStruct((n, d), BF16),
    )
    return pl.pallas_call(
        functools.partial(_in_proj_kernel, widths=widths),
        grid=(n // tm,),
        in_specs=[row(d), _const_spec((1, d)), _const_spec(w_cat.shape),
                  _const_spec((1, LANES)), _const_spec((1, LANES)), _const_spec((1, LANES)),
                  _const_spec((1, LANES)), tab, tab],
        out_specs=(row(pool_w), row(q_w), row(HEAD_DIM), row(HEAD_DIM), row(HEAD_DIM), row(HEAD_DIM),
                   row(d), row(d)),
        out_shape=out_shape,
        compiler_params=_cparams(1),
        name="in_proj",
    )(x2d, g_mix[None, :], w_cat, qg, qgp, kg, kgp, cos, sin)


def _attn_kernel(q_ref, k0_ref, k1_ref, v0_ref, v1_ref, o_ref, *, ck):
    grp = N_HEADS // N_KV_HEADS
    q = q_ref[0]
    tq = q.shape[0]
    n_chunks = k0_ref.shape[1] // ck
    for g, (k_ref, v_ref) in enumerate(((k0_ref, v0_ref), (k1_ref, v1_ref))):
        qs = jnp.concatenate(
            [q[:, (g * grp + j) * HEAD_DIM:(g * grp + j + 1) * HEAD_DIM] for j in range(grp)], axis=0)

        def scores(j):
            return lax.dot_general(qs, k_ref[0, j * ck:(j + 1) * ck, :], (((1,), (1,)), ((), ())),
                                   preferred_element_type=F32)

        m = jnp.max(scores(0), axis=-1, keepdims=True)
        for j in range(1, n_chunks):
            m = jnp.maximum(m, jnp.max(scores(j), axis=-1, keepdims=True))
        l = jnp.zeros_like(m)
        acc = jnp.zeros((qs.shape[0], HEAD_DIM), F32)
        for j in range(n_chunks):
            e = jnp.exp2(scores(j) - m)
            l = l + jnp.sum(e, axis=-1, keepdims=True)
            acc = acc + jnp.dot(e.astype(BF16), v_ref[0, j * ck:(j + 1) * ck, :], preferred_element_type=F32)
        o = (acc / l).astype(BF16)
        for j in range(grp):
            h = g * grp + j
            o_ref[0, :, h * HEAD_DIM:(h + 1) * HEAD_DIM] = o[j * tq:(j + 1) * tq]


def _attention(q, k0, k1, v0, v1, *, tq):
    b, t, qw = q.shape
    kv = pl.BlockSpec((1, t, HEAD_DIM), lambda bi, i: (bi, 0, 0))
    qs = pl.BlockSpec((1, tq, qw), lambda bi, i: (bi, i, 0))
    return pl.pallas_call(
        functools.partial(_attn_kernel, ck=_pick(t, 512)),
        grid=(b, t // tq),
        in_specs=[qs, kv, kv, kv, kv],
        out_specs=qs,
        out_shape=jax.ShapeDtypeStruct((b, t, qw), BF16),
        compiler_params=_cparams(2),
        name="attention",
    )(q, k0, k1, v0, v1)


def _merge_kernel(x_ref, p_ref, prev_ref, next_ref, attn_ref, gp_ref, ga_ref, pw_ref, ps_ref,
                  wpo_ref, wao_ref, wo_ref, o_ref, e_ref, *, nb_seq, seq):
    i = pl.program_id(0)
    tm = x_ref.shape[0]
    blk = i % nb_seq
    halo = POOL_HALO
    e_ref[0:halo, :] = jnp.where(blk == 0, 0.0, prev_ref[...])
    e_ref[halo:halo + tm, :] = p_ref[...]
    e_ref[halo + tm:, :] = jnp.where(blk == nb_seq - 1, 0.0, next_ref[...])

    t = blk * tm + lax.broadcasted_iota(jnp.int32, (tm, 1), 0)
    mixed = []
    for gi, w in enumerate(POOL_WINDOWS):
        cols = slice(gi * POOL_GROUP, (gi + 1) * POOL_GROUP)
        acc = None
        for dlt in range(-(w // 2), w - w // 2):
            piece = e_ref[halo + dlt:halo + dlt + tm, cols]
            acc = piece if acc is None else acc + piece
        cnt = (jnp.minimum(t + (w - w // 2), seq) - jnp.maximum(t - w // 2, 0)).astype(F32)
        pooled = (acc / cnt - e_ref[halo:halo + tm, cols]).astype(BF16)
        mixed.append(jnp.dot(pooled, pw_ref[gi], preferred_element_type=F32) * ps_ref[:, cols])
    mixed = jnp.concatenate(mixed, axis=-1).astype(BF16)
    pool_branch = jnp.dot(mixed, wpo_ref[...], preferred_element_type=F32)
    attn_branch = jnp.dot(attn_ref[...], wao_ref[...], preferred_element_type=F32)
    merged = gp_ref[...].astype(F32) * pool_branch + ga_ref[...].astype(F32) * attn_branch
    o_ref[...] = x_ref[...] + jnp.dot(merged.astype(BF16), wo_ref[...], preferred_element_type=F32)


def _merge(x2d, seq, p, attn, gp, ga, pool_w, pool_scale, w_pool_out, w_attn_out, w_out, *, tm):
    n, d = x2d.shape
    pw = p.shape[1]
    nb_seq = seq // tm
    hb = tm // POOL_HALO
    n_halo_blocks = n // POOL_HALO
    row = lambda w: pl.BlockSpec((tm, w), lambda i: (i, 0))
    prev = pl.BlockSpec((POOL_HALO, pw), lambda i: (jnp.maximum(i * hb - 1, 0), 0))
    nxt = pl.BlockSpec((POOL_HALO, pw), lambda i: (jnp.minimum((i + 1) * hb, n_halo_blocks - 1), 0))
    return pl.pallas_call(
        functools.partial(_merge_kernel, nb_seq=nb_seq, seq=seq),
        grid=(n // tm,),
        in_specs=[row(d), row(pw), prev, nxt, row(attn.shape[1]), row(d), row(d),
                  _const_spec(pool_w.shape), _const_spec((1, pw)), _const_spec(w_pool_out.shape),
                  _const_spec(w_attn_out.shape), _const_spec(w_out.shape)],
        out_specs=row(d),
        out_shape=jax.ShapeDtypeStruct((n, d), F32),
        scratch_shapes=[pltpu.VMEM((tm + 2 * POOL_HALO, pw), F32)],
        compiler_params=_cparams(1),
        name="merge",
    )(x2d, p, p, p, attn, gp, ga, pool_w.astype(BF16), pool_scale[None, :], w_pool_out.astype(BF16),
      w_attn_out.astype(BF16), w_out.astype(BF16))


def _topk_rows(vals, k, payload=None):
    n = vals.shape[0]
    iota = lax.broadcasted_iota(jnp.int32, vals.shape, 0).astype(F32)
    cur = vals
    out_v, out_p = [], []
    for _ in range(k):
        m = jnp.max(cur, axis=0, keepdims=True)
        pos = jnp.min(jnp.where(cur == m, iota, float(n)), axis=0, keepdims=True)
        hit = iota == pos
        out_v.append(m)
        if payload is None:
            out_p.append(pos)
        else:
            out_p.append(jnp.sum(jnp.where(hit, payload, 0.0), axis=0, keepdims=True))
        cur = jnp.where(hit, NEG_INF, cur)
    return jnp.concatenate(out_v, axis=0), jnp.concatenate(out_p, axis=0)


def _candidate_pairs():
    return [(a, b) for a in range(PEER_TOPK) for b in range(PEER_TOPK) if (a + 1) * (b + 1) <= PEER_TOPK]


def _pair_selectors():
    pairs = _candidate_pairs()
    n = len(pairs) + (-len(pairs) % SUBLANES)
    rows = jnp.arange(len(pairs))
    sel_a = jnp.zeros((n, PEER_TOPK), F32).at[rows, jnp.array([p[0] for p in pairs])].set(1.0)
    sel_b = jnp.zeros((n, PEER_TOPK), F32).at[rows, jnp.array([p[1] for p in pairs])].set(1.0)
    pad = jnp.where(jnp.arange(n) < len(pairs), 0.0, NEG_INF).astype(F32)[:, None]
    return sel_a, sel_b, pad


def _select_rows(sel, x):
    return jnp.dot(sel, x, precision=lax.Precision.HIGHEST, preferred_element_type=F32)


def _peer_route_kernel(x_ref, g_ref, wq_ref, keys_ref, sela_ref, selb_ref, pad_ref, rows_ref, gate_ref):
    x = x_ref[...]
    r = lax.rsqrt(jnp.mean(x * x, axis=-1, keepdims=True) + EPS)
    h = ((x * r) * g_ref[...]).astype(BF16)
    sel_a = sela_ref[...]
    sel_b = selb_ref[...]
    n_tiles = x.shape[0] // LANES
    row_list = [[] for _ in range(n_tiles)]
    gate_rows = [[] for _ in range(n_tiles)]
    for hd in range(PEER_HEADS):
        sv = [[] for _ in range(n_tiles)]
        si = [[] for _ in range(n_tiles)]
        for part in range(2):
            c0 = (hd * 2 + part) * PEER_HALF
            qh = jnp.dot(h, wq_ref[:, c0:c0 + PEER_HALF], preferred_element_type=F32).astype(BF16)
            s_t = lax.dot_general(keys_ref[hd * 2 + part], qh, (((1,), (1,)), ((), ())),
                                  preferred_element_type=F32)
            for lt in range(n_tiles):
                v, ix = _topk_rows(s_t[:, lt * LANES:(lt + 1) * LANES], PEER_TOPK)
                sv[lt].append(v)
                si[lt].append(ix)
        for lt in range(n_tiles):
            cand = (_select_rows(sel_a, sv[lt][0]) + _select_rows(sel_b, sv[lt][1])) + pad_ref[...]
            cidx = _select_rows(sel_a, si[lt][0]) * float(N_KEYS) + _select_rows(sel_b, si[lt][1])
            top_s, idx = _topk_rows(cand, PEER_TOPK, payload=cidx)
            e = jnp.exp(top_s - top_s[0:1])
            gate_rows[lt].append(e / jnp.sum(e, axis=0, keepdims=True))
            row_list[lt].append(idx)
    for lt in range(n_tiles):
        tok = slice(lt * LANES, (lt + 1) * LANES)
        rows_ref[tok, :] = (jnp.concatenate(row_list[lt], axis=0).astype(jnp.int32) * ROW_WORDS).T
        gate_ref[tok, :] = jnp.concatenate(gate_rows[lt], axis=0).T


def _peer_route(x1, g_ffn, w_query, sub_keys, *, tb):
    n, d = x1.shape
    nk = PEER_HEADS * PEER_TOPK
    keys = sub_keys.reshape(PEER_HEADS * 2, N_KEYS, PEER_HALF).astype(BF16)
    sel_a, sel_b, pad = _pair_selectors()
    row = lambda w: pl.BlockSpec((tb, w), lambda i: (i, 0))
    return pl.pallas_call(
        _peer_route_kernel,
        grid=(n // tb,),
        in_specs=[row(d), _const_spec((1, d)), _const_spec(w_query.shape), _const_spec(keys.shape),
                  _const_spec(sel_a.shape), _const_spec(sel_b.shape), _const_spec(pad.shape)],
        out_specs=(row(nk), row(nk)),
        out_shape=(jax.ShapeDtypeStruct((n, nk), jnp.int32), jax.ShapeDtypeStruct((n, nk), F32)),
        compiler_params=_cparams(1),
        name="peer_route",
    )(x1, g_ffn[None, :], w_query.astype(BF16), keys, sel_a, sel_b, pad)


ROW_WORDS = 4


def _pack_table(tbl):
    e, d = tbl.shape
    bits = lax.bitcast_convert_type(tbl.astype(BF16), jnp.uint16).astype(jnp.uint32)
    half = d // 2
    words = bits[:, :half] | (bits[:, half:] << 16)
    return lax.bitcast_convert_type(words, jnp.int32).reshape(e * ROW_WORDS, LANES)


def _unpack(w):
    lo = lax.bitcast_convert_type(w << 16, F32)
    hi = lax.bitcast_convert_type(w & jnp.int32(-65536), F32)
    return lo, hi


def _load_row_pair(tbl, row_a, row_b):
    wa = tbl[pl.ds(pl.multiple_of(row_a, ROW_WORDS), ROW_WORDS), :]
    wb = tbl[pl.ds(pl.multiple_of(row_b, ROW_WORDS), ROW_WORDS), :]
    return jnp.concatenate([wa, wb], axis=0)


def _resident_spec(shape):
    nd = len(shape)
    return pl.BlockSpec(shape, lambda *_: (0,) * nd, pipeline_mode=pl.Buffered(1))


ROW_GROUP = 8


def _row_offset_scratch(nk):
    return [pltpu.SMEM((2, ROW_GROUP, nk), jnp.int32), pltpu.SemaphoreType.DMA((2,))]


def _for_each_token(rows_vmem, idx_smem, sem, token):
    n_groups = rows_vmem.shape[0] // ROW_GROUP

    def fetch(group, slot):
        return pltpu.make_async_copy(rows_vmem.at[pl.ds(group * ROW_GROUP, ROW_GROUP)], idx_smem.at[slot],
                                     sem.at[slot])

    def run_group(group, slot):
        for u in range(ROW_GROUP):
            token(group * ROW_GROUP + u, lambda r, u=u: idx_smem[slot, u, r])

    fetch(0, 0).start()

    def trip(j, carry):
        g0 = 2 * j
        fetch(g0 + 1, 1).start()
        fetch(g0, 0).wait()
        run_group(g0, 0)
        ahead = jnp.minimum(g0 + 2, n_groups - 1)
        fetch(ahead, 0).start()
        fetch(g0 + 1, 1).wait()
        run_group(g0 + 1, 1)
        return carry

    lax.fori_loop(0, n_groups // 2, trip, 0)
    fetch(n_groups - 1, 0).wait()


def _peer_up_kernel(rows_ref, x_ref, g_ref, gate_ref, ones_ref, eye_ref, tbl, hid_ref,
                    h_s, s_s, a_s, idx_smem, sem):
    x = x_ref[...]
    ss = jnp.sum(jnp.sum(x * x, axis=2, keepdims=True), axis=1, keepdims=True)
    h_s[...] = (x * lax.rsqrt(ss * (1.0 / (SUBLANES * LANES)) + EPS)) * g_ref[...]
    tb = x.shape[0]
    nk = rows_ref.shape[1]

    def token(c, row_of):
        xv = h_s[c]
        xl = jnp.concatenate([xv[:ROW_WORDS]] * 2, axis=0)
        xh = jnp.concatenate([xv[ROW_WORDS:]] * 2, axis=0)
        for r in range(0, nk, 2):
            lo, hi = _unpack(_load_row_pair(tbl, row_of(r), row_of(r + 1)))
            s_s[r * ROW_WORDS:(r + 2) * ROW_WORDS, :] = lo * xl + hi * xh
        part = s_s[pl.ds(0, nk, stride=ROW_WORDS), :]
        for j in range(1, ROW_WORDS):
            part = part + s_s[pl.ds(j, nk, stride=ROW_WORDS), :]
        a_s[c] = part

    _for_each_token(rows_ref, idx_smem, sem, token)
    ones = ones_ref[...]
    tc = min(tb, 16)
    for c0 in range(0, tb, tc):
        part = a_s[c0:c0 + tc].reshape(tc * nk, LANES)
        p_hi = part.astype(BF16)
        p_lo = (part - p_hi.astype(F32)).astype(BF16)
        sums = (jnp.dot(p_hi, ones, preferred_element_type=F32)
                + jnp.dot(p_lo, ones, preferred_element_type=F32)).reshape(tc, nk, LANES)
        a = jnp.sum(sums * eye_ref[...], axis=1)
        hid_ref[c0:c0 + tc, :] = (0.5 * a * (1.0 + lax.erf(a * (2.0 ** -0.5)))) * gate_ref[c0:c0 + tc, :]


def _peer_up(x1r, g_ffn, rows, gate, tbl_u, *, tb):
    n = x1r.shape[0]
    nk = rows.shape[1]
    assert nk == LANES and tb % (2 * ROW_GROUP) == 0
    return pl.pallas_call(
        _peer_up_kernel,
        grid=(n // tb,),
        in_specs=[pl.BlockSpec((tb, nk), lambda i: (i, 0)),
                  pl.BlockSpec((tb, SUBLANES, LANES), lambda i: (i, 0, 0)),
                  _const_spec((SUBLANES, LANES)),
                  pl.BlockSpec((tb, nk), lambda i: (i, 0)),
                  _const_spec((LANES, LANES)), _const_spec((nk, LANES)),
                  _resident_spec(tbl_u.shape)],
        out_specs=pl.BlockSpec((tb, nk), lambda i: (i, 0)),
        out_shape=jax.ShapeDtypeStruct((n, nk), F32),
        scratch_shapes=[pltpu.VMEM((tb, SUBLANES, LANES), F32),
                        pltpu.VMEM((nk * ROW_WORDS, LANES), F32),
                        pltpu.VMEM((tb, nk, LANES), F32)] + _row_offset_scratch(nk),
        compiler_params=_cparams(1),
        name="peer_up",
    )(rows, x1r, g_ffn.reshape(SUBLANES, LANES), gate, jnp.ones((LANES, LANES), BF16),
      jnp.eye(nk, LANES, dtype=F32), tbl_u)


def _peer_down_kernel(rows_ref, hid_ref, x_ref, g_ref, tbl, y_ref, w_s, o_s, idx_smem, sem):
    tb = x_ref.shape[0]
    nk = rows_ref.shape[1]
    n_acc = 4
    hid_t = hid_ref[...].T
    for c in range(tb):
        w_s[c] = jnp.broadcast_to(hid_t[:, c:c + 1], (nk, LANES))

    first_row = lax.broadcasted_iota(jnp.int32, (SUBLANES, LANES), 0) < ROW_WORDS

    def token(c, row_of):
        acc_lo = [jnp.zeros((SUBLANES, LANES), F32) for _ in range(n_acc)]
        acc_hi = [jnp.zeros((SUBLANES, LANES), F32) for _ in range(n_acc)]
        for r in range(0, nk, 2):
            lo, hi = _unpack(_load_row_pair(tbl, row_of(r), row_of(r + 1)))
            wgt = jnp.where(first_row, jnp.broadcast_to(w_s[c, r:r + 1, :], (SUBLANES, LANES)),
                            jnp.broadcast_to(w_s[c, r + 1:r + 2, :], (SUBLANES, LANES)))
            k = (r // 2) % n_acc
            acc_lo[k] = acc_lo[k] + wgt * lo
            acc_hi[k] = acc_hi[k] + wgt * hi
        lo = (acc_lo[0] + acc_lo[1]) + (acc_lo[2] + acc_lo[3])
        hi = (acc_hi[0] + acc_hi[1]) + (acc_hi[2] + acc_hi[3])
        o_s[c] = x_ref[c] + jnp.concatenate([lo[:ROW_WORDS] + lo[ROW_WORDS:], hi[:ROW_WORDS] + hi[ROW_WORDS:]],
                                            axis=0)

    _for_each_token(rows_ref, idx_smem, sem, token)
    xo = o_s[...]
    ss = jnp.sum(jnp.sum(xo * xo, axis=2, keepdims=True), axis=1, keepdims=True)
    y_ref[...] = (xo * lax.rsqrt(ss * (1.0 / (SUBLANES * LANES)) + EPS)) * g_ref[...]


def _peer_down(x1r, g_final, rows, hid, tbl_v, *, tb):
    n = x1r.shape[0]
    nk = rows.shape[1]
    tok = pl.BlockSpec((tb, SUBLANES, LANES), lambda i: (i, 0, 0))
    return pl.pallas_call(
        _peer_down_kernel,
        grid=(n // tb,),
        in_specs=[pl.BlockSpec((tb, nk), lambda i: (i, 0)), pl.BlockSpec((tb, nk), lambda i: (i, 0)), tok,
                  _const_spec((SUBLANES, LANES)), _resident_spec(tbl_v.shape)],
        out_specs=tok,
        out_shape=jax.ShapeDtypeStruct((n, SUBLANES, LANES), F32),
        scratch_shapes=[pltpu.VMEM((tb, nk, LANES), F32),
                        pltpu.VMEM((tb, SUBLANES, LANES), F32)] + _row_offset_scratch(nk),
        compiler_params=_cparams(1),
        name="peer_down",
    )(rows, hid, x1r, g_final.reshape(SUBLANES, LANES), tbl_v)


def _pick(n, pref):
    t = min(pref, n)
    while n % t:
        t //= 2
    return t


def _trunk(x, g_mix, w_in, pool_w, pool_scale, q_norm, k_norm, w_pool_out, w_attn_out, w_out,
           g_ffn, w_query, sub_keys, tbl_u, tbl_v, g_final):
    b, t, d = x.shape
    n = b * t
    x2d = x.reshape(n, d)
    tm = _pick(t, 512)
    p, q, k0, k1, v0, v1, gp, ga = _in_proj(x2d, t, g_mix, w_in, q_norm, k_norm, tm=tm)
    r3 = lambda a: a.reshape(b, t, a.shape[-1])
    attn = _attention(r3(q), r3(k0), r3(k1), r3(v0), r3(v1), tq=_pick(t, 128))
    x1 = _merge(x2d, t, p, attn.reshape(n, -1), gp, ga, pool_w, pool_scale, w_pool_out, w_attn_out,
                w_out, tm=tm)
    rows, gate = _peer_route(x1, g_ffn, w_query, sub_keys, tb=_pick(n, 256))
    x1r = x1.reshape(n, SUBLANES, LANES)
    tbp = _pick(n, 128)
    hid = _peer_up(x1r, g_ffn, rows, gate, tbl_u, tb=tbp)
    y = _peer_down(x1r, g_final, rows, hid, tbl_v, tb=tbp)
    return y.reshape(b, t, d)


def kernel(x_prompt, x_sample, g_mix, w_in, pool_w, pool_scale, q_norm, k_norm, w_pool_out, w_attn_out,
           w_out, g_ffn, w_query, sub_keys, expert_u, expert_v, g_final):
    assert g_mix.shape[0] == 1, "single-layer trunk"
    tbl_u = _pack_table(expert_u[0])
    tbl_v = _pack_table(expert_v[0])
    args = (g_mix[0], w_in[0], pool_w[0], pool_scale[0], q_norm[0], k_norm[0], w_pool_out[0],
            w_attn_out[0], w_out[0], g_ffn[0], w_query[0], sub_keys[0], tbl_u, tbl_v, g_final)
    return (_trunk(x_prompt, *args), _trunk(x_sample, *args))
```
